```python
import functools
import jax, jax.numpy as jnp
from jax import lax
import numpy as np

D_MODEL = 1024
BATCH = 2
SEQ = 8192
DEPTH = 1
DEC_BATCH = 128
DEC_SEQ = 8
PAST_LEN = 8192
PAGE_SIZE = 128

N_META = 16
HEAD_DIM = 64
RWKV_HEADS = 16
RWKV_WIDTH = RWKV_HEADS * HEAD_DIM
SB_HEADS = 16
SB_WIDTH = SB_HEADS * HEAD_DIM
DECAY_LORA = 64
A_LORA = 64
GATE_LORA = 128
P_RWKV = 3 * RWKV_WIDTH + DECAY_LORA + A_LORA + GATE_LORA
P_SB = 3 * SB_WIDTH
P_GATE = 2 * D_MODEL
P_IN = P_RWKV + P_SB + P_GATE
Q_BLOCK = 128
SB_BIAS_MIN = -12.0
N_EXPERTS = 256
TOP_K = 8
EXPERT_HIDDEN = 256
SHARED_HIDDEN = 256
N_EXPERT_GROUPS = 8
TOPK_GROUPS = 4
ROUTED_SCALE = 2.5
MOE_BLOCK = 64
ALPHA = (2.0 * DEPTH) ** 0.25
BETA_INIT = (8.0 * DEPTH) ** -0.25
LN_EPS = 1e-5
GN_EPS = 64e-5

kernel_name = "rwkv7_stickbreak_moe_hybrid_step"


def _layer_norm(x, g, b):
    xf = x.astype(jnp.float32)
    mu = xf.mean(-1, keepdims=True)
    var = jnp.square(xf - mu).mean(-1, keepdims=True)
    return ((xf - mu) * lax.rsqrt(var + LN_EPS) * g + b).astype(x.dtype)


def _wkv_step(S, inp):
    r, w, k, v, a, b = inp
    sa = jnp.einsum('bhvk,bhk->bhv', S, a)
    S = S * w[:, :, None, :] + sa[..., None] * b[:, :, None, :] + v[..., None] * k[:, :, None, :]
    return S, jnp.einsum('bhvk,bhk->bhv', S, r)


def _rwkv7_branch(p, shift_prev, wkv_prev, mu_shift, w0, w_decay_up, a0, w_a_up, w_g_up, k_k, k_a, r_k, gn_g, gn_b):
    bsz, t = p.shape[:2]
    f32 = jnp.float32
    p_prev = jnp.concatenate([shift_prev[:, None, :].astype(p.dtype), p[:, :-1]], axis=1)
    ps = p + (p_prev - p) * mu_shift
    idx = (RWKV_WIDTH, 2 * RWKV_WIDTH, 3 * RWKV_WIDTH, 3 * RWKV_WIDTH + DECAY_LORA, 3 * RWKV_WIDTH + DECAY_LORA + A_LORA)
    r, k, v, lw, la, lg = jnp.split(ps, idx, axis=-1)
    w_log = -jax.nn.softplus(-(w0 + jnp.tanh(lw) @ w_decay_up)) - 0.5
    decay = jnp.exp(-jnp.exp(w_log.astype(f32)))
    a = jax.nn.sigmoid((a0 + la @ w_a_up).astype(f32))
    g = jax.nn.sigmoid(lg) @ w_g_up
    heads = lambda z: z.astype(f32).reshape(bsz, t, RWKV_HEADS, HEAD_DIM)
    r, k, v, a, decay = heads(r), heads(k), heads(v), heads(a), heads(decay)
    kk = k * k_k
    kk = kk / jnp.maximum(jnp.sqrt(jnp.sum(kk * kk, -1, keepdims=True)), 1e-12)
    k = k * (1.0 + (a - 1.0) * k_a)
    xs = tuple(jnp.moveaxis(z, 1, 0) for z in (r, decay, k, v, -kk, kk * a))
    wkv_final, y = lax.scan(_wkv_step, wkv_prev.astype(f32), xs)
    y = jnp.moveaxis(y, 0, 1)
    mu = y.mean(-1, keepdims=True)
    var = jnp.square(y - mu).mean(-1, keepdims=True)
    y = ((y - mu) * lax.rsqrt(var + GN_EPS)).reshape(bsz, t, RWKV_WIDTH) * gn_g + gn_b
    bonus = (jnp.sum(r * k * r_k, -1, keepdims=True) * v).reshape(bsz, t, RWKV_WIDTH)
    y = (y + bonus) * g
    return y.astype(p.dtype), p[:, -1], wkv_final.astype(wkv_prev.dtype)


def _stick_breaking(z, vis, suffix0):
    u = jnp.where(vis, jax.nn.log_sigmoid(-z), 0.0)
    log_surv = suffix0[..., None] + lax.cumsum(u, axis=z.ndim - 1, reverse=True) - u
    A = jnp.where(vis, jnp.exp(jax.nn.log_sigmoid(z) + log_surv), 0.0)
    return A, suffix0 + u.sum(-1)


def _sb_logits(q, k, sb_bias):
    z = jnp.einsum('bqhd,bkhd->bhqk', q, k, preferred_element_type=jnp.float32) * (HEAD_DIM ** -0.5)
    return z + sb_bias.astype(jnp.float32)[None, :, None, None]


def _sb_prompt(q, k, v, sb_bias):
    bsz, L = q.shape[:2]
    k_pos = jnp.arange(L)
    vf = v.astype(jnp.float32)

    def block(q_blk, q_pos):
        z = _sb_logits(q_blk, k, sb_bias)
        vis = k_pos[None, :] < q_pos[:, None]
        A, _ = _stick_breaking(z, vis, jnp.zeros(z.shape[:-1], jnp.float32))
        return jnp.einsum('bhqk,bkhd->bqhd', A, vf)

    o_meta = block(q[:, :N_META], jnp.arange(N_META))
    n_real = L - N_META
    n_blk = n_real // Q_BLOCK
    q_real = q[:, N_META:].reshape(bsz, n_blk, Q_BLOCK, SB_HEADS, HEAD_DIM).transpose(1, 0, 2, 3, 4)
    pos_real = (N_META + jnp.arange(n_real)).reshape(n_blk, Q_BLOCK)
    o_real = lax.map(lambda args: block(*args), (q_real, pos_real))
    o_real = o_real.transpose(1, 0, 2, 3, 4).reshape(bsz, n_real, SB_HEADS, HEAD_DIM)
    return jnp.concatenate([o_meta, o_real], axis=1).astype(q.dtype)


def _sb_sample(q, k, v, sb_bias, cache_k, cache_v, page_table, layer):
    t = q.shape[1]
    pos = jnp.arange(t)
    z = _sb_logits(q, k, sb_bias)
    A, suf = _stick_breaking(z, pos[None, :] < pos[:, None], jnp.zeros(z.shape[:-1], jnp.float32))
    o = jnp.einsum('bhqk,bkhd->bqhd', A, v.astype(jnp.float32))

    def page_step(carry, phys):
        o_acc, suffix = carry
        kp = cache_k[layer, phys]
        vp = cache_v[layer, phys]
        zp = _sb_logits(q, kp, sb_bias)
        Ap, suffix = _stick_breaking(zp, True, suffix)
        o_acc = o_acc + jnp.einsum('bhqk,bkhd->bqhd', Ap, vp.astype(jnp.float32))
        return (o_acc, suffix), None

    (o, _), _ = lax.scan(page_step, (o, suf), jnp.flip(page_table, axis=1).T)
    return o.astype(q.dtype)


def _route(x2, w_router, router_bias):
    s = jax.nn.sigmoid((x2 @ w_router).astype(jnp.float32))
    biased = s + router_bias
    grp = biased.reshape(-1, N_EXPERT_GROUPS, N_EXPERTS // N_EXPERT_GROUPS)
    group_score = lax.top_k(grp, 2)[0].sum(-1)
    _, top_groups = lax.top_k(group_score, TOPK_GROUPS)
    group_mask = (top_groups[..., None] == jnp.arange(N_EXPERT_GROUPS)).any(1)
    expert_mask = jnp.repeat(group_mask, N_EXPERTS // N_EXPERT_GROUPS, axis=1)
    _, top_idx = lax.top_k(jnp.where(expert_mask, biased, -jnp.inf), TOP_K)
    sel = jnp.take_along_axis(s, top_idx, axis=1)
    return top_idx, sel / sel.sum(-1, keepdims=True) * ROUTED_SCALE


def _routed_experts(x2, top_idx, weights, w_e_gate, w_e_up, w_e_down):
    n_tok = x2.shape[0]
    n_assign = n_tok * TOP_K
    n_blocks = -(-n_assign // MOE_BLOCK) + N_EXPERTS
    e_flat = top_idx.reshape(-1)
    tok_flat = jnp.repeat(jnp.arange(n_tok, dtype=jnp.int32), TOP_K)
    w_flat = weights.reshape(-1)
    order = jnp.argsort(e_flat)
    e_sorted = e_flat[order]
    counts = jnp.zeros((N_EXPERTS,), jnp.int32).at[e_flat].add(1)
    padded = (counts + MOE_BLOCK - 1) // MOE_BLOCK * MOE_BLOCK
    pad_end = jnp.cumsum(padded)
    pad_start = pad_end - padded
    start = jnp.cumsum(counts) - counts
    slot = pad_start[e_sorted] + jnp.arange(n_assign, dtype=jnp.int32) - start[e_sorted]
    n_slots = n_blocks * MOE_BLOCK
    slot_tok = jnp.full((n_slots,), n_tok, jnp.int32).at[slot].set(tok_flat[order])
    slot_w = jnp.zeros((n_slots,), jnp.float32).at[slot].set(w_flat[order])
    block_expert = jnp.minimum(jnp.searchsorted(pad_end, jnp.arange(n_blocks, dtype=jnp.int32) * MOE_BLOCK, side='right'), N_EXPERTS - 1)
    x_pad = jnp.concatenate([x2, jnp.zeros((1, D_MODEL), x2.dtype)], axis=0)

    def step(out, blk):
        tok, wt, e = blk
        xb = x_pad[tok]
        h = jax.nn.silu(xb @ w_e_gate[e]) * (xb @ w_e_up[e])
        return out.at[tok].add((h @ w_e_down[e]).astype(jnp.float32) * wt[:, None]), None

    out, _ = lax.scan(step, jnp.zeros((n_tok + 1, D_MODEL), jnp.float32),
                      (slot_tok.reshape(n_blocks, MOE_BLOCK), slot_w.reshape(n_blocks, MOE_BLOCK), block_expert))
    return out[:n_tok].astype(x2.dtype)


def _moe(x2, w_router, router_bias, w_e_gate, w_e_up, w_e_down, w_s_gate, w_s_up, w_s_down):
    top_idx, weights = _route(x2, w_router, router_bias)
    shared = (jax.nn.silu(x2 @ w_s_gate) * (x2 @ w_s_up)) @ w_s_down
    return shared + _routed_experts(x2, top_idx, weights, w_e_gate, w_e_up, w_e_down)


def _layer(x, shift_prev, wkv_prev, sb_fn, rw, mix_w, moe_w):
    w_in, w_proj_a, w_proj_b, w_o, ln1_g, ln1_b = mix_w
    bsz, t, _ = x.shape
    p = x @ w_in
    qkv = p[..., P_RWKV:P_RWKV + P_SB].reshape(bsz, t, 3, SB_HEADS, HEAD_DIM)
    q, k, v = qkv[:, :, 0], qkv[:, :, 1], qkv[:, :, 2]
    gate = jax.nn.sigmoid(p[..., P_RWKV + P_SB:])
    y_a, shift_new, wkv_new = _rwkv7_branch(p[..., :P_RWKV], shift_prev, wkv_prev, *rw)
    y_b = sb_fn(q, k, v).reshape(bsz, t, SB_WIDTH)
    merged = gate[..., :D_MODEL] * (y_a @ w_proj_a) + gate[..., D_MODEL:] * (y_b @ w_proj_b)
    x = _layer_norm(ALPHA * x + merged @ w_o, ln1_g, ln1_b)
    ffn = _moe(x.reshape(-1, D_MODEL), *moe_w[:-2]).reshape(x.shape)
    x = _layer_norm(ALPHA * x + ffn, moe_w[-2], moe_w[-1])
    return x, k, v, shift_new, wkv_new


def setup_inputs(seed: int = 0) -> dict:
    key = jax.random.key(seed)
    ks = iter(jax.random.split(key, 48))
    nrm = lambda shape, scale=1.0: jax.random.normal(next(ks), shape, jnp.float32) * scale
    n_pages = PAST_LEN // PAGE_SIZE
    n_used = DEC_BATCH * n_pages
    n_pool = n_used + n_used // 4
    L = DEPTH
    page_table = jax.random.permutation(next(ks), n_pool)[:n_used].reshape(DEC_BATCH, n_pages).astype(jnp.int32)
    sb_bias = jnp.broadcast_to(jnp.linspace(0.0, SB_BIAS_MIN, SB_HEADS, dtype=jnp.float32), (L, SB_HEADS)) + nrm((L, SB_HEADS), 0.01)
    return {
        "x_prompt": nrm((BATCH, SEQ, D_MODEL)),
        "x_sample": nrm((DEC_BATCH, DEC_SEQ, D_MODEL)),
        "cache_k": nrm((L, n_pool, PAGE_SIZE, SB_HEADS, HEAD_DIM)),
        "cache_v": nrm((L, n_pool, PAGE_SIZE, SB_HEADS, HEAD_DIM)),
        "page_table": page_table,
        "state_wkv": nrm((L, DEC_BATCH, RWKV_HEADS, HEAD_DIM, HEAD_DIM), 0.3),
        "state_shift": nrm((L, DEC_BATCH, P_RWKV)),
        "meta_tokens": nrm((N_META, D_MODEL)),
        "w_in": nrm((L, D_MODEL, P_IN), D_MODEL ** -0.5),
        "mu_shift": jax.random.uniform(next(ks), (L, P_RWKV), jnp.float32),
        "w0": jax.random.uniform(next(ks), (L, RWKV_WIDTH), jnp.float32, -6.0, -1.0),
        "w_decay_up": nrm((L, DECAY_LORA, RWKV_WIDTH), 0.5 * DECAY_LORA ** -0.5),
        "a0": nrm((L, RWKV_WIDTH), 0.1),
        "w_a_up": nrm((L, A_LORA, RWKV_WIDTH), 0.5 * A_LORA ** -0.5),
        "w_g_up": nrm((L, GATE_LORA, RWKV_WIDTH), GATE_LORA ** -0.5),
        "k_k": 0.85 + nrm((L, RWKV_HEADS, HEAD_DIM), 0.02),
        "k_a": 1.0 + nrm((L, RWKV_HEADS, HEAD_DIM), 0.02),
        "r_k": nrm((L, RWKV_HEADS, HEAD_DIM), 0.1),
        "gn_g": 1.0 + nrm((L, RWKV_WIDTH), 0.02),
        "gn_b": nrm((L, RWKV_WIDTH), 0.02),
        "sb_bias": sb_bias,
        "w_proj_a": nrm((L, RWKV_WIDTH, D_MODEL), RWKV_WIDTH ** -0.5),
        "w_proj_b": nrm((L, SB_WIDTH, D_MODEL), SB_WIDTH ** -0.5),
        "w_o": nrm((L, D_MODEL, D_MODEL), BETA_INIT * D_MODEL ** -0.5),
        "ln1_g": 1.0 + nrm((L, D_MODEL), 0.02),
        "ln1_b": nrm((L, D_MODEL), 0.02),
        "w_router": nrm((L, D_MODEL, N_EXPERTS), D_MODEL ** -0.5),
        "router_bias": nrm((L, N_EXPERTS), 0.01),
        "w_e_gate": nrm((L, N_EXPERTS, D_MODEL, EXPERT_HIDDEN), D_MODEL ** -0.5),
        "w_e_up": nrm((L, N_EXPERTS, D_MODEL, EXPERT_HIDDEN), D_MODEL ** -0.5),
        "w_e_down": nrm((L, N_EXPERTS, EXPERT_HIDDEN, D_MODEL), BETA_INIT * EXPERT_HIDDEN ** -0.5),
        "w_s_gate": nrm((L, D_MODEL, SHARED_HIDDEN), D_MODEL ** -0.5),
        "w_s_up": nrm((L, D_MODEL, SHARED_HIDDEN), D_MODEL ** -0.5),
        "w_s_down": nrm((L, SHARED_HIDDEN, D_MODEL), BETA_INIT * SHARED_HIDDEN ** -0.5),
        "ln2_g": 1.0 + nrm((L, D_MODEL), 0.02),
        "ln2_b": nrm((L, D_MODEL), 0.02),
    }


def reference(x_prompt, x_sample, cache_k, cache_v, page_table, state_wkv, state_shift, meta_tokens,
              w_in, mu_shift, w0, w_decay_up, a0, w_a_up, w_g_up, k_k, k_a, r_k, gn_g, gn_b, sb_bias,
              w_proj_a, w_proj_b, w_o, ln1_g, ln1_b, w_router, router_bias, w_e_gate, w_e_up, w_e_down,
              w_s_gate, w_s_up, w_s_down, ln2_g, ln2_b):
    bsz = x_prompt.shape[0]
    meta = jnp.broadcast_to(meta_tokens[None].astype(x_prompt.dtype), (bsz, N_META, D_MODEL))
    xp = jnp.concatenate([meta, x_prompt], axis=1)
    xs = x_sample
    k_p, v_p, k_s, v_s, wkv_p, shift_p, wkv_s, shift_s = [], [], [], [], [], [], [], []
    for l in range(DEPTH):
        rw = (mu_shift[l], w0[l], w_decay_up[l], a0[l], w_a_up[l], w_g_up[l], k_k[l], k_a[l], r_k[l], gn_g[l], gn_b[l])
        mix_w = (w_in[l], w_proj_a[l], w_proj_b[l], w_o[l], ln1_g[l], ln1_b[l])
        moe_w = (w_router[l], router_bias[l], w_e_gate[l], w_e_up[l], w_e_down[l],
                 w_s_gate[l], w_s_up[l], w_s_down[l], ln2_g[l], ln2_b[l])
        zero_shift = jnp.zeros((bsz, P_RWKV), xp.dtype)
        zero_wkv = jnp.zeros((bsz, RWKV_HEADS, HEAD_DIM, HEAD_DIM), state_wkv.dtype)
        sb_prompt_fn = functools.partial(_sb_prompt, sb_bias=sb_bias[l])
        xp, kp_l, vp_l, sp_l, wp_l = _layer(xp, zero_shift, zero_wkv, sb_prompt_fn, rw, mix_w, moe_w)
        sb_sample_fn = functools.partial(_sb_sample, sb_bias=sb_bias[l], cache_k=cache_k, cache_v=cache_v,
                                         page_table=page_table, layer=l)
        xs, ks_l, vs_l, ss_l, ws_l = _layer(xs, state_shift[l], state_wkv[l], sb_sample_fn, rw, mix_w, moe_w)
        k_p.append(kp_l); v_p.append(vp_l); shift_p.append(sp_l); wkv_p.append(wp_l)
        k_s.append(ks_l); v_s.append(vs_l); shift_s.append(ss_l); wkv_s.append(ws_l)
    y_prompt = xp[:, N_META:]
    y_sample = xs
    k_prompt_new = jnp.stack(k_p)
    v_prompt_new = jnp.stack(v_p)
    k_sample_new = jnp.stack(k_s)
    v_sample_new = jnp.stack(v_s)
    wkv_prompt_new = jnp.stack(wkv_p)
    shift_prompt_new = jnp.stack(shift_p)
    wkv_sample_new = jnp.stack(wkv_s)
    shift_sample_new = jnp.stack(shift_s)
    return (y_prompt, y_sample, k_prompt_new, v_prompt_new, k_sample_new, v_sample_new,
            wkv_prompt_new, shift_prompt_new, wkv_sample_new, shift_sample_new)
```

```python
import functools

import jax
import jax.numpy as jnp
from jax import lax
from jax.experimental import pallas as pl
from jax.experimental.pallas import tpu as pltpu

F32 = jnp.float32
BF16 = jnp.bfloat16

D_MODEL = 1024
N_META = 16
HEAD_DIM = 64
N_HEADS = 16
WIDTH = N_HEADS * HEAD_DIM
DECAY_LORA = 64
A_LORA = 64
GATE_LORA = 128
LORA = DECAY_LORA + A_LORA + GATE_LORA
P_RWKV = 3 * WIDTH + LORA
P_SB = 3 * WIDTH
P_GATE = 2 * D_MODEL
P_IN = P_RWKV + P_SB + P_GATE
PAGE_SIZE = 128
N_EXPERTS = 256
TOP_K = 8
EXPERT_HIDDEN = 256
N_EXPERT_GROUPS = 8
GROUP_SIZE = N_EXPERTS // N_EXPERT_GROUPS
TOPK_GROUPS = 4
ROUTED_SCALE = 2.5
DEPTH = 1
ALPHA = (2.0 * DEPTH) ** 0.25
LN_EPS = 1e-5
GN_EPS = 64e-5

COL_RWKV = 0
COL_SB = 3 * WIDTH
COL_GATE = COL_SB + P_SB
COL_LORA = COL_GATE + P_GATE

LANE = 128
VMEM_LIMIT = 56 * 1024 * 1024


def _cparams(sem):
    return pltpu.CompilerParams(dimension_semantics=sem, vmem_limit_bytes=VMEM_LIMIT)


def _split(x):
    hi = x.astype(BF16)
    lo = (x - hi.astype(F32)).astype(BF16)
    return hi, lo


_NN = (((1,), (0,)), ((), ()))
_NT = (((1,), (1,)), ((), ()))
_TN = (((0,), (0,)), ((), ()))


def _dot(a, b, dims=_NN, passes=1):
    dg = lambda x, y: lax.dot_general(x, y, dims, preferred_element_type=F32)
    if passes == 1:
        return dg(a.astype(BF16), b.astype(BF16))
    ah, al = _split(a)
    bh, bl = _split(b)
    return dg(ah, bh) + (dg(ah, bl) + dg(al, bh))


def _dot_exact_rhs(a, b_bf16, dims=_NN):
    ah, al = _split(a)
    dg = lambda x, y: lax.dot_general(x, y, dims, preferred_element_type=F32)
    return dg(ah, b_bf16) + dg(al, b_bf16)


def _dot_exact_lhs(a_bf16, b, dims=_NN):
    bh, bl = _split(b)
    dg = lambda x, y: lax.dot_general(x, y, dims, preferred_element_type=F32)
    return dg(a_bf16, bh) + dg(a_bf16, bl)


def _softplus(x):
    return jnp.maximum(x, 0.0) + jnp.log1p(jnp.exp(-jnp.abs(x)))


def _sigmoid(x):
    return 1.0 / (1.0 + jnp.exp(-x))


def _layer_norm(x, g, b):
    mu = jnp.mean(x, axis=-1, keepdims=True)
    xc = x - mu
    var = jnp.mean(xc * xc, axis=-1, keepdims=True)
    return xc * lax.rsqrt(var + LN_EPS) * g + b


def _proj_kernel(x_ref, w_ref, o_ref):
    o_ref[...] = jnp.dot(x_ref[...].astype(BF16), w_ref[...], preferred_element_type=F32)


def _project(x, w_bf16, tm, tn):
    n, k = x.shape
    m = w_bf16.shape[1]
    return pl.pallas_call(
        _proj_kernel,
        out_shape=jax.ShapeDtypeStruct((n, m), F32),
        grid=(n // tm, m // tn),
        in_specs=[pl.BlockSpec((tm, k), lambda i, j: (i, 0)), pl.BlockSpec((k, tn), lambda i, j: (0, j))],
        out_specs=pl.BlockSpec((tm, tn), lambda i, j: (i, j)),
        compiler_params=_cparams(("parallel", "parallel")),
        name="in_proj",
    )(x, w_bf16)


def _rwkv_pre_kernel(pr, pk, pv, pl_, qr, qk, qv, ql, valid, mur, muk, muv, mul, w0, wd, a0, wa, wg,
                     r_o, lw_o, k_o, v_o, a_o, g_o):
    ok = valid[...]
    mix = lambda p, q, mu: (p[...] + (q[...] - p[...]) * mu[...])
    r_o[...] = mix(pr, qr, mur) * ok
    k_o[...] = mix(pk, qk, muk) * ok
    v_o[...] = mix(pv, qv, muv) * ok
    lo = mix(pl_, ql, mul)
    w_log = -_softplus(-(w0[...] + _dot(jnp.tanh(lo), wd[...], passes=3))) - 0.5
    lw_o[...] = -jnp.exp(w_log) * ok
    a_o[...] = _sigmoid(a0[...] + _dot(lo, wa[...], passes=3))
    g_o[...] = _dot(_sigmoid(lo), wg[...], passes=3)


def _rwkv_pre(p, p_prev, valid, mu_shift, w0, wd_pad, a0, wa_pad, wg_pad, tm):
    n = p.shape[0]
    row = lambda w, c: pl.BlockSpec((tm, w), lambda i, c=c: (i, c))
    vec = lambda w, c: pl.BlockSpec((1, w), lambda i, c=c: (0, c))
    full = lambda a: pl.BlockSpec(a.shape, lambda i: (0,) * a.ndim)
    rkv = [row(WIDTH, 0), row(WIDTH, 1), row(WIDTH, 2)]
    prev_lora = 3 * WIDTH // LORA
    in_specs = rkv + [row(LORA, COL_LORA // LORA)] + rkv + [row(LORA, prev_lora)] + [
        pl.BlockSpec((tm, 1), lambda i: (i, 0)),
        vec(WIDTH, 0), vec(WIDTH, 1), vec(WIDTH, 2), vec(LORA, prev_lora),
        full(w0), full(wd_pad), full(a0), full(wa_pad), full(wg_pad)]
    out = jax.ShapeDtypeStruct((n, WIDTH), F32)
    return pl.pallas_call(
        _rwkv_pre_kernel,
        out_shape=(out,) * 6,
        grid=(n // tm,),
        in_specs=in_specs,
        out_specs=(pl.BlockSpec((tm, WIDTH), lambda i: (i, 0)),) * 6,
        compiler_params=_cparams(("parallel",)),
        name="rwkv_pre",
    )(p, p, p, p, p_prev, p_prev, p_prev, p_prev, valid, mu_shift, mu_shift, mu_shift, mu_shift,
      w0, wd_pad, a0, wa_pad, wg_pad)


def _wkv_kernel(r_ref, lw_ref, k_ref, v_ref, a_ref, g_ref, s0_ref, kk_ref, ka_ref, rk_ref, gg_ref, gb_ref,
                y_ref, s_ref, *, chunk, heads, passes):
    c = pl.program_id(2)

    @pl.when(c == 0)
    def _():
        s_ref[...] = s0_ref[...]

    ti = lax.broadcasted_iota(jnp.int32, (chunk, chunk), 0)
    tj = lax.broadcasted_iota(jnp.int32, (chunk, chunk), 1)
    incl = ti >= tj
    strict = ti > tj
    tri_incl = jnp.where(incl, 1.0, 0.0).astype(BF16)
    eye = jnp.where(ti == tj, 1.0, 0.0)
    mm = functools.partial(_dot, passes=passes)

    for h in range(heads):
        sl = slice(h * HEAD_DIM, (h + 1) * HEAD_DIM)
        r, lw, k, v, a, g = (ref[:, sl] for ref in (r_ref, lw_ref, k_ref, v_ref, a_ref, g_ref))
        kk = k * kk_ref[:, sl]
        kk = kk / jnp.maximum(jnp.sqrt(jnp.sum(kk * kk, axis=-1, keepdims=True)), 1e-12)
        k2 = k * (1.0 + (a - 1.0) * ka_ref[:, sl])
        b_vec = kk * a
        cl = _dot_exact_lhs(tri_incl, lw)
        cl_end = cl[chunk - 1:chunk, :]
        p_in = jnp.exp(cl)
        p_inv = jnp.exp(-cl)
        p_end = jnp.exp(cl_end - cl)
        at = -kk * jnp.exp(cl - lw)
        rt = r * p_in
        bt = b_vec * p_inv
        kt = k2 * p_inv
        l_ab = jnp.where(strict, mm(at, bt, _NT), 0.0)
        l_ak = jnp.where(strict, mm(at, kt, _NT), 0.0)
        m_rb = jnp.where(incl, mm(rt, bt, _NT), 0.0)
        m_rk = jnp.where(incl, mm(rt, kt, _NT), 0.0)
        inv = eye + l_ab
        pw = l_ab
        n = 2
        while n < chunk:
            pw = mm(pw, pw)
            inv = inv + mm(inv, pw)
            n *= 2
        s0 = s_ref[0, h]
        u = mm(inv, mm(at, s0, _NT) + mm(l_ak, v))
        y = mm(rt, s0, _NT) + mm(m_rb, u) + mm(m_rk, v)
        s_ref[0, h] = s0 * jnp.exp(cl_end) + mm(u, b_vec * p_end, _TN) + mm(v, k2 * p_end, _TN)
        mu = jnp.mean(y, axis=-1, keepdims=True)
        yc = y - mu
        var = jnp.mean(yc * yc, axis=-1, keepdims=True)
        yn = yc * lax.rsqrt(var + GN_EPS) * gg_ref[:, sl] + gb_ref[:, sl]
        bonus = jnp.sum(r * k2 * rk_ref[:, sl], axis=-1, keepdims=True) * v
        y_ref[:, sl] = (yn + bonus) * g


def _wkv_scan(r, lw, k, v, a, g, s0, k_k, k_a, r_k, gn_g, gn_b, *, row0, seq_stride, n_seq, n_chunks, chunk,
              heads, passes):
    n_rows = n_seq * seq_stride
    hb = N_HEADS // heads
    wblk = heads * HEAD_DIM
    blk0, blk_stride = row0 // chunk, seq_stride // chunk
    row = pl.BlockSpec((chunk, wblk), lambda s, j, c: (blk0 + s * blk_stride + c, j))
    vec = pl.BlockSpec((1, wblk), lambda s, j, c: (0, j))
    st = pl.BlockSpec((1, heads, HEAD_DIM, HEAD_DIM), lambda s, j, c: (s, j, 0, 0))
    y, s_new = pl.pallas_call(
        functools.partial(_wkv_kernel, chunk=chunk, heads=heads, passes=passes),
        out_shape=(jax.ShapeDtypeStruct((n_rows, WIDTH), F32), jax.ShapeDtypeStruct(s0.shape, F32)),
        grid=(n_seq, hb, n_chunks),
        in_specs=[row] * 6 + [st] + [vec] * 5,
        out_specs=(pl.BlockSpec((chunk, wblk), lambda s, j, c: (s * blk_stride + c, j)), st),
        compiler_params=_cparams(("parallel", "parallel", "arbitrary")),
        name="wkv_scan",
    )(r, lw, k, v, a, g, s0, k_k, k_a, r_k, gn_g, gn_b)
    return y, s_new


def _sb_tile(z, vis, tri, suffix):
    sp = _softplus(z)
    u = -sp
    if vis is not None:
        u = jnp.where(vis, u, 0.0)
    later = _dot_exact_rhs(u, tri)
    w = jnp.exp((z - sp) + (suffix + later))
    if vis is not None:
        w = jnp.where(vis, w, 0.0)
    return w, suffix + jnp.sum(u, axis=-1, keepdims=True)


def _later_key_matrix(n):
    m = lax.broadcasted_iota(jnp.int32, (n, n), 0)
    j = lax.broadcasted_iota(jnp.int32, (n, n), 1)
    return jnp.where(m > j, 1.0, 0.0).astype(BF16), j < m


def _sb_prompt_kernel(bias_ref, q_ref, k_ref, v_ref, o_ref, *, tile):
    hp, qi = pl.program_id(1), pl.program_id(2)
    tri, causal = _later_key_matrix(tile)
    lane = lax.broadcasted_iota(jnp.int32, (1, LANE), 1)
    q = q_ref[...] * (HEAD_DIM ** -0.5)
    heads = []
    for h in range(2):
        own = (lane >= h * HEAD_DIM) & (lane < (h + 1) * HEAD_DIM)
        heads.append((jnp.where(own, q, 0.0).astype(BF16), bias_ref[2 * hp + h]))

    def step(j, vis, carry):
        kb = k_ref[pl.ds(pl.multiple_of(j * tile, tile), tile), :].astype(BF16)
        vb = v_ref[pl.ds(pl.multiple_of(j * tile, tile), tile), :].astype(BF16)
        out = []
        for (qh, bias), (suffix, acc) in zip(heads, carry):
            z = lax.dot_general(qh, kb, _NT, preferred_element_type=F32) + bias
            w, suffix = _sb_tile(z, vis, tri, suffix)
            out.append((suffix, acc + jnp.dot(w.astype(BF16), vb, preferred_element_type=F32)))
        return tuple(out)

    zero = (jnp.zeros((tile, 1), F32), jnp.zeros((tile, LANE), F32))
    carry = step(qi, causal, (zero, zero))
    carry = lax.fori_loop(0, qi, lambda t, c: step(qi - 1 - t, None, c), carry)
    o_ref[...] = jnp.where(lane < HEAD_DIM, carry[0][1], carry[1][1])


def _sb_prompt(p, sb_bias, *, n_seq, seq_stride, tile):
    n_q = seq_stride // tile
    qc, kc, vc = ((COL_SB + i * WIDTH) // LANE for i in range(3))
    grid_spec = pltpu.PrefetchScalarGridSpec(
        num_scalar_prefetch=1,
        grid=(n_seq, N_HEADS // 2, n_q),
        in_specs=[pl.BlockSpec((tile, LANE), lambda b, j, i, bias: (b * n_q + i, qc + j)),
                  pl.BlockSpec((seq_stride, LANE), lambda b, j, i, bias: (b, kc + j)),
                  pl.BlockSpec((seq_stride, LANE), lambda b, j, i, bias: (b, vc + j))],
        out_specs=pl.BlockSpec((tile, LANE), lambda b, j, i, bias: (b * n_q + i, j)),
    )
    return pl.pallas_call(
        functools.partial(_sb_prompt_kernel, tile=tile),
        out_shape=jax.ShapeDtypeStruct((n_seq * seq_stride, WIDTH), F32),
        grid_spec=grid_spec,
        compiler_params=_cparams(("parallel", "parallel", "arbitrary")),
        name="sb_prompt",
    )(sb_bias, p, p, p)


def _sb_sample_kernel(pt_ref, q_ref, kn_ref, vn_ref, bias_ref, *rest, n_pg, t_new):
    k_pages, v_pages = rest[:n_pg], rest[n_pg:2 * n_pg]
    o_ref, acc_ref, suf_ref = rest[2 * n_pg:]
    s = pl.program_id(1)
    rows = N_HEADS * t_new
    tri, _ = _later_key_matrix(PAGE_SIZE)
    rh = lax.broadcasted_iota(jnp.int32, (rows, WIDTH), 0) // t_new
    ch = lax.broadcasted_iota(jnp.int32, (rows, WIDTH), 1) // HEAD_DIM
    own = rh == ch
    q = q_ref[...] * (HEAD_DIM ** -0.5)
    q_bd = jnp.where(own, jnp.concatenate([q] * N_HEADS, axis=0), 0.0).astype(BF16)
    bias = bias_ref[...]

    def page(kp, k_dims, vp, v_dims, vis):
        z = lax.dot_general(q_bd, kp.astype(BF16), k_dims, preferred_element_type=F32) + bias
        w, suf = _sb_tile(z, vis, tri, suf_ref[...])
        suf_ref[...] = suf
        acc_ref[...] += lax.dot_general(w.astype(BF16), vp.astype(BF16), v_dims, preferred_element_type=F32)

    @pl.when(s == 0)
    def _():
        acc_ref[...] = jnp.zeros_like(acc_ref)
        suf_ref[...] = jnp.zeros_like(suf_ref)
        pad = jnp.zeros((PAGE_SIZE - t_new, WIDTH), F32)
        key = lax.broadcasted_iota(jnp.int32, (rows, PAGE_SIZE), 1)
        qpos = lax.broadcasted_iota(jnp.int32, (rows, PAGE_SIZE), 0) % t_new
        page(jnp.concatenate([kn_ref[...], pad], axis=0), _NT, jnp.concatenate([vn_ref[...], pad], axis=0), _NN,
             key < qpos)

    for i in range(n_pg):
        page(k_pages[i][...], _NN, v_pages[i][...], _NT, None)

    @pl.when(s == pl.num_programs(1) - 1)
    def _():
        acc = jnp.where(own, acc_ref[...], 0.0)
        out = acc[0:t_new]
        for h in range(1, N_HEADS):
            out = out + acc[h * t_new:(h + 1) * t_new]
        o_ref[...] = out


def _sb_sample(q, k_new, v_new, bias_rows, cache_k, cache_v, page_table, *, n_pg):
    n_seq, n_pages = page_table.shape
    t_new = q.shape[0] // n_seq
    rows = N_HEADS * t_new
    tok = pl.BlockSpec((t_new, WIDTH), lambda b, s, pt: (b, 0))

    def page_spec(i):
        return pl.BlockSpec((None, WIDTH, PAGE_SIZE), lambda b, s, pt, i=i: (pt[b, n_pages - 1 - (s * n_pg + i)], 0, 0))

    grid_spec = pltpu.PrefetchScalarGridSpec(
        num_scalar_prefetch=1,
        grid=(n_seq, n_pages // n_pg),
        in_specs=[tok, tok, tok, pl.BlockSpec((rows, 1), lambda b, s, pt: (0, 0))]
        + [page_spec(i) for i in range(n_pg)] * 2,
        out_specs=tok,
        scratch_shapes=[pltpu.VMEM((rows, WIDTH), F32), pltpu.VMEM((rows, 1), F32)],
    )
    return pl.pallas_call(
        functools.partial(_sb_sample_kernel, n_pg=n_pg, t_new=t_new),
        out_shape=jax.ShapeDtypeStruct(q.shape, F32),
        grid_spec=grid_spec,
        compiler_params=_cparams(("parallel", "arbitrary")),
        name="sb_sample",
    )(page_table, q, k_new, v_new, bias_rows, *([cache_k] * n_pg), *([cache_v] * n_pg))


def _merge_kernel(x_ref, ya_ref, yb_ref, ga_ref, gb_ref, wa_ref, wb_ref, wo_ref, g_ref, b_ref, o_ref):
    ma = _sigmoid(ga_ref[...]) * _dot(ya_ref[...], wa_ref[...])
    mb = _sigmoid(gb_ref[...]) * _dot(yb_ref[...], wb_ref[...])
    mixed = _dot(ma + mb, wo_ref[...])
    o_ref[...] = _layer_norm(ALPHA * x_ref[...] + mixed, g_ref[...], b_ref[...])


def _merge(x, y_a, y_b, p, w_proj_a, w_proj_b, w_o, ln_g, ln_b, tm):
    n = x.shape[0]
    row = lambda c: pl.BlockSpec((tm, D_MODEL), lambda i, c=c: (i, c))
    full = lambda a: pl.BlockSpec(a.shape, lambda i: (0,) * a.ndim)
    gate_blk = COL_GATE // D_MODEL
    return pl.pallas_call(
        _merge_kernel,
        out_shape=jax.ShapeDtypeStruct((n, D_MODEL), F32),
        grid=(n // tm,),
        in_specs=[row(0), row(0), row(0), row(gate_blk), row(gate_blk + 1),
                  full(w_proj_a), full(w_proj_b), full(w_o), full(ln_g), full(ln_b)],
        out_specs=row(0),
        compiler_params=_cparams(("parallel",)),
        name="merge_ln1",
    )(x, y_a, y_b, p, p, w_proj_a, w_proj_b, w_o, ln_g, ln_b)


def _router_kernel(x_ref, wr_ref, bias_ref, idx_ref, wgt_ref):
    tm = x_ref.shape[0]
    logits = _dot(wr_ref[...], x_ref[...], _NT, passes=3)
    s = _sigmoid(logits)
    biased = s + bias_ref[...]
    neg = -jnp.inf
    eid = lax.broadcasted_iota(jnp.int32, (N_EXPERTS, tm), 0)
    gid = eid // GROUP_SIZE

    def take_max(cand, ids, n_ids):
        m = jnp.max(cand, axis=0, keepdims=True)
        first = jnp.min(jnp.where(cand == m, ids, n_ids), axis=0, keepdims=True)
        return m, first

    gids = lax.broadcasted_iota(jnp.int32, (N_EXPERT_GROUPS, tm), 0)
    gs = jnp.full((N_EXPERT_GROUPS, tm), neg, F32)
    for gidx in range(N_EXPERT_GROUPS):
        blk = biased[gidx * GROUP_SIZE:(gidx + 1) * GROUP_SIZE]
        ids = lax.broadcasted_iota(jnp.int32, (GROUP_SIZE, tm), 0) + gidx * GROUP_SIZE
        m1, i1 = take_max(blk, ids, N_EXPERTS)
        m2, _ = take_max(jnp.where(ids == i1, neg, blk), ids, N_EXPERTS)
        gs = jnp.where(gids == gidx, m1 + m2, gs)
    cand = jnp.full((N_EXPERTS, tm), neg, F32)
    for _ in range(TOPK_GROUPS):
        _, gi = take_max(gs, gids, N_EXPERT_GROUPS)
        gs = jnp.where(gids == gi, neg, gs)
        cand = jnp.where(gid == gi, biased, cand)
    kids = lax.broadcasted_iota(jnp.int32, (TOP_K, tm), 0)
    idx = jnp.zeros((TOP_K, tm), jnp.int32)
    sel = jnp.zeros((TOP_K, tm), F32)
    for j in range(TOP_K):
        _, ei = take_max(cand, eid, N_EXPERTS)
        hit = eid == ei
        idx = jnp.where(kids == j, ei, idx)
        sel = jnp.where(kids == j, jnp.sum(jnp.where(hit, s, 0.0), axis=0, keepdims=True), sel)
        cand = jnp.where(hit, neg, cand)
    idx_ref[...] = idx
    wgt_ref[...] = sel / jnp.sum(sel, axis=0, keepdims=True) * ROUTED_SCALE


def _router(x, w_router_t, bias_col, tm):
    n = x.shape[0]
    full = lambda a: pl.BlockSpec(a.shape, lambda i: (0,) * a.ndim)
    out = pl.BlockSpec((TOP_K, tm), lambda i: (0, i))
    return pl.pallas_call(
        _router_kernel,
        out_shape=(jax.ShapeDtypeStruct((TOP_K, n), jnp.int32), jax.ShapeDtypeStruct((TOP_K, n), F32)),
        grid=(n // tm,),
        in_specs=[pl.BlockSpec((tm, D_MODEL), lambda i: (i, 0)), full(w_router_t), full(bias_col)],
        out_specs=(out, out),
        compiler_params=_cparams(("parallel",)),
        name="router",
    )(x, w_router_t, bias_col)


def _experts_kernel(be_ref, src_ref, dst_ref, x_hbm, wt_ref, wg_ref, wu_ref, wd_ref, y_hbm, xin, yout, sem_in,
                    sem_out, *, tb):
    blk = pl.program_id(0)
    n_used = be_ref[be_ref.shape[0] - 1]

    @pl.when(blk < n_used)
    def _():
        def gather(i, _):
            pltpu.make_async_copy(x_hbm.at[pl.ds(src_ref[0, i], 1)], xin.at[pl.ds(i, 1)], sem_in).start()
            return 0

        lax.fori_loop(0, tb, gather, 0)
        pltpu.make_async_copy(x_hbm.at[pl.ds(0, tb)], xin, sem_in).wait()
        xb = xin[...].astype(BF16)
        hg = jnp.dot(xb, wg_ref[...].astype(BF16), preferred_element_type=F32)
        hu = jnp.dot(xb, wu_ref[...].astype(BF16), preferred_element_type=F32)
        hidden = hg * _sigmoid(hg) * hu
        y = jnp.dot(hidden.astype(BF16), wd_ref[...].astype(BF16), preferred_element_type=F32)
        yout[...] = y * wt_ref[...]

        def scatter(i, _):
            pltpu.make_async_copy(yout.at[pl.ds(i, 1)], y_hbm.at[pl.ds(dst_ref[0, i], 1)], sem_out).start()
            return 0

        lax.fori_loop(0, tb, scatter, 0)
        pltpu.make_async_copy(yout, y_hbm.at[pl.ds(0, tb)], sem_out).wait()


def _experts(x, block_expert, slot_src, slot_dst, slot_w, w_e_gate, w_e_up, w_e_down, *, tb, n_rows_out):
    n_blocks = slot_src.shape[0]
    wspec = lambda a: pl.BlockSpec((None,) + a.shape[1:], lambda i, be: (be[i], 0, 0))
    ids = pl.BlockSpec((None, 1, tb), lambda i, be: (i, 0, 0), memory_space=pltpu.SMEM)
    grid_spec = pltpu.PrefetchScalarGridSpec(
        num_scalar_prefetch=1,
        grid=(n_blocks,),
        in_specs=[ids, ids, pl.BlockSpec(memory_space=pl.ANY),
                  pl.BlockSpec((tb, 1), lambda i, be: (i, 0)),
                  wspec(w_e_gate), wspec(w_e_up), wspec(w_e_down)],
        out_specs=pl.BlockSpec(memory_space=pl.ANY),
        scratch_shapes=[pltpu.VMEM((tb, D_MODEL), F32), pltpu.VMEM((tb, D_MODEL), F32),
                        pltpu.SemaphoreType.DMA(()), pltpu.SemaphoreType.DMA(())],
    )
    return pl.pallas_call(
        functools.partial(_experts_kernel, tb=tb),
        out_shape=jax.ShapeDtypeStruct((n_rows_out, D_MODEL), F32),
        grid_spec=grid_spec,
        compiler_params=_cparams(("arbitrary",)),
        name="experts",
    )(block_expert, slot_src, slot_dst, x, slot_w, w_e_gate, w_e_up, w_e_down)


def _dispatch(top_idx, top_w, n_tok, tb):
    n_assign = TOP_K * n_tok
    n_blocks = -(-n_assign // tb) + N_EXPERTS
    n_slots = n_blocks * tb
    e_flat = top_idx.reshape(-1)
    order = jnp.argsort(e_flat).astype(jnp.int32)
    e_sorted = e_flat[order]
    counts = jnp.zeros((N_EXPERTS,), jnp.int32).at[e_flat].add(1)
    padded = (counts + tb - 1) // tb * tb
    pad_end = jnp.cumsum(padded)
    pad_start = pad_end - padded
    start = jnp.cumsum(counts) - counts
    slot = pad_start[e_sorted] + jnp.arange(n_assign, dtype=jnp.int32) - start[e_sorted]
    spare = n_assign + jnp.arange(n_slots, dtype=jnp.int32)
    slot_src = jnp.zeros((n_slots,), jnp.int32).at[slot].set(order % n_tok).reshape(n_blocks, 1, tb)
    slot_dst = spare.at[slot].set(order).reshape(n_blocks, 1, tb)
    slot_w = jnp.zeros((n_slots,), F32).at[slot].set(top_w.reshape(-1)[order])
    blk_start = jnp.arange(n_blocks, dtype=jnp.int32) * tb
    block_expert = jnp.minimum(jnp.searchsorted(pad_end, blk_start, side="right"), N_EXPERTS - 1).astype(jnp.int32)
    n_used = (pad_end[-1] // tb).astype(jnp.int32)
    return jnp.concatenate([block_expert, n_used[None]]), slot_src, slot_dst, slot_w.reshape(n_slots, 1), n_slots


def _ffn_out_kernel(x_ref, wg_ref, wu_ref, wd_ref, g_ref, b_ref, *rest):
    yr_refs, o_ref = rest[:-1], rest[-1]
    x = x_ref[...]
    xb = x.astype(BF16)
    hg = jnp.dot(xb, wg_ref[...], preferred_element_type=F32)
    hu = jnp.dot(xb, wu_ref[...], preferred_element_type=F32)
    ffn = jnp.dot((hg * _sigmoid(hg) * hu).astype(BF16), wd_ref[...], preferred_element_type=F32)
    for yr_ref in yr_refs:
        ffn = ffn + yr_ref[...]
    o_ref[...] = _layer_norm(ALPHA * x + ffn, g_ref[...], b_ref[...])


def _ffn_out(x, y_routed, w_s_gate, w_s_up, w_s_down, ln_g, ln_b, tm):
    n = x.shape[0]
    full = lambda a: pl.BlockSpec(a.shape, lambda i: (0,) * a.ndim)
    row = pl.BlockSpec((tm, D_MODEL), lambda i: (i, 0))
    slab = lambda j: pl.BlockSpec((tm, D_MODEL), lambda i, j=j: (j * (n // tm) + i, 0))
    return pl.pallas_call(
        _ffn_out_kernel,
        out_shape=jax.ShapeDtypeStruct((n, D_MODEL), F32),
        grid=(n // tm,),
        in_specs=[row, full(w_s_gate), full(w_s_up), full(w_s_down), full(ln_g), full(ln_b)]
        + [slab(j) for j in range(TOP_K)],
        out_specs=row,
        compiler_params=_cparams(("parallel",)),
        name="ffn_out",
    )(x, w_s_gate, w_s_up, w_s_down, ln_g, ln_b, *([y_routed] * TOP_K))


WKV_CHUNK = 64
SB_TILE = 128
SEQ_PAD = 256
ROW_TILE = 512
EXPERT_ROWS = 128
ROUTER_TILE = 256
SAMPLE_PAGES_PER_STEP = 8


def _pages_keys_minor(cache):
    n_pool = cache.shape[0]
    return jnp.transpose(cache, (0, 2, 3, 1)).reshape(n_pool, WIDTH, PAGE_SIZE)


def _shift_rows(p_rwkv, first):
    return jnp.concatenate([first[:, None, :], p_rwkv[:, :-1]], axis=1)


def kernel(x_prompt, x_sample, cache_k, cache_v, page_table, state_wkv, state_shift, meta_tokens, w_in, mu_shift, w0,
           w_decay_up, a0, w_a_up, w_g_up, k_k, k_a, r_k, gn_g, gn_b, sb_bias, w_proj_a, w_proj_b, w_o, ln1_g, ln1_b,
           w_router, router_bias, w_e_gate, w_e_up, w_e_down, w_s_gate, w_s_up, w_s_down, ln2_g, ln2_b):
    assert w_in.shape[0] == DEPTH == 1
    n_p, seq = x_prompt.shape[:2]
    n_s, t_s = x_sample.shape[:2]
    lp = N_META + seq
    lp_pad = -(-lp // SEQ_PAD) * SEQ_PAD
    rows_p = n_p * lp_pad
    rows_s = n_s * t_s
    n_tok = rows_p + rows_s
    assert lp_pad % WKV_CHUNK == 0 and lp_pad % SB_TILE == 0 and n_tok % ROW_TILE == 0 and rows_p % t_s == 0

    meta = jnp.broadcast_to(meta_tokens[None].astype(F32), (n_p, N_META, D_MODEL))
    xp = jnp.concatenate([meta, x_prompt, jnp.zeros((n_p, lp_pad - lp, D_MODEL), F32)], axis=1)
    x = jnp.concatenate([xp.reshape(rows_p, D_MODEL), x_sample.reshape(rows_s, D_MODEL)], axis=0)
    valid = jnp.concatenate([jnp.broadcast_to((jnp.arange(lp_pad) < lp)[None], (n_p, lp_pad)).reshape(rows_p),
                             jnp.ones((rows_s,), jnp.bool_)]).astype(F32)[:, None]

    w = w_in[0]
    w_cols = jnp.concatenate([w[:, :3 * WIDTH], w[:, P_RWKV:], w[:, 3 * WIDTH:P_RWKV]], axis=1)
    p = _project(x, w_cols.astype(BF16), ROW_TILE, 768)
    p_rwkv = jnp.concatenate([p[:, :3 * WIDTH], p[:, COL_LORA:]], axis=1)
    pp3 = p_rwkv[:rows_p].reshape(n_p, lp_pad, P_RWKV)
    ps3 = p_rwkv[rows_p:].reshape(n_s, t_s, P_RWKV)

    p_prev = jnp.concatenate([
        _shift_rows(pp3, jnp.zeros((n_p, P_RWKV), F32)).reshape(rows_p, P_RWKV),
        _shift_rows(ps3, state_shift[0]).reshape(rows_s, P_RWKV)], axis=0)
    zpad = lambda w, lo: jnp.zeros((LORA, WIDTH), F32).at[lo:lo + w.shape[0]].set(w)
    vec = lambda a: a.reshape(1, -1)
    r, lw, k, v, a, g = _rwkv_pre(p, p_prev, valid, mu_shift, vec(w0[0]), zpad(w_decay_up[0], 0), vec(a0[0]),
                                  zpad(w_a_up[0], DECAY_LORA), zpad(w_g_up[0], DECAY_LORA + A_LORA), ROW_TILE)
    head_vecs = (vec(k_k[0]), vec(k_a[0]), vec(r_k[0]), vec(gn_g[0]), vec(gn_b[0]))
    ya_p, wkv_p = _wkv_scan(r, lw, k, v, a, g, jnp.zeros((n_p,) + state_wkv.shape[2:], F32), *head_vecs,
                            row0=0, seq_stride=lp_pad, n_seq=n_p, n_chunks=lp_pad // WKV_CHUNK, chunk=WKV_CHUNK,
                            heads=4, passes=3)
    ya_s, wkv_s = _wkv_scan(r, lw, k, v, a, g, state_wkv[0], *head_vecs,
                            row0=rows_p, seq_stride=t_s, n_seq=n_s, n_chunks=1, chunk=t_s, heads=N_HEADS, passes=3)
    y_a = jnp.concatenate([ya_p, ya_s], axis=0)

    qkv_s = p[rows_p:, COL_SB:COL_GATE].reshape(rows_s, 3, WIDTH)
    yb_p = _sb_prompt(p, sb_bias[0], n_seq=n_p, seq_stride=lp_pad, tile=SB_TILE)
    yb_s = _sb_sample(qkv_s[:, 0], qkv_s[:, 1], qkv_s[:, 2], jnp.repeat(sb_bias[0], t_s)[:, None],
                      _pages_keys_minor(cache_k[0]), _pages_keys_minor(cache_v[0]),
                      page_table, n_pg=SAMPLE_PAGES_PER_STEP)
    y_b = jnp.concatenate([yb_p, yb_s], axis=0)

    x1 = _merge(x, y_a, y_b, p, w_proj_a[0].astype(BF16), w_proj_b[0].astype(BF16), w_o[0].astype(BF16),
                vec(ln1_g[0]), vec(ln1_b[0]), ROW_TILE)

    top_idx, top_w = _router(x1, w_router[0].T, router_bias[0][:, None], ROUTER_TILE)
    block_expert, slot_src, slot_dst, slot_w, n_slots = _dispatch(top_idx, top_w, n_tok, EXPERT_ROWS)
    y_routed = _experts(x1, block_expert, slot_src, slot_dst, slot_w, w_e_gate[0], w_e_up[0], w_e_down[0],
                        tb=EXPERT_ROWS, n_rows_out=TOP_K * n_tok + n_slots)
    x2 = _ffn_out(x1, y_routed, w_s_gate[0].astype(BF16), w_s_up[0].astype(BF16), w_s_down[0].astype(BF16),
                  vec(ln2_g[0]), vec(ln2_b[0]), ROW_TILE)

    x2p = x2[:rows_p].reshape(n_p, lp_pad, D_MODEL)
    qkv_p = p[:rows_p, COL_SB:COL_GATE].reshape(n_p, lp_pad, 3, N_HEADS, HEAD_DIM)[:, :lp]
    qkv_s5 = qkv_s.reshape(n_s, t_s, 3, N_HEADS, HEAD_DIM)
    return (x2p[:, N_META:lp], x2[rows_p:].reshape(n_s, t_s, D_MODEL),
            qkv_p[None, :, :, 1], qkv_p[None, :, :, 2], qkv_s5[None, :, :, 1], qkv_s5[None, :, :, 2],
            wkv_p[None], pp3[:, lp - 1][None], wkv_s[None], ps3[:, t_s - 1][None])
```

```python
import functools

import jax
import jax.numpy as jnp
from jax import lax
from jax.experimental import pallas as pl
from jax.experimental.pallas import tpu as pltpu

F32 = jnp.float32
BF16 = jnp.bfloat16

D_MODEL = 1024
N_META = 16
HEAD_DIM = 64
N_HEADS = 16
WIDTH = N_HEADS * HEAD_DIM
DECAY_LORA = 64
A_LORA = 64
GATE_LORA = 128
LORA = DECAY_LORA + A_LORA + GATE_LORA
P_RWKV = 3 * WIDTH + LORA
P_SB = 3 * WIDTH
P_GATE = 2 * D_MODEL
P_IN = P_RWKV + P_SB + P_GATE
PAGE_SIZE = 128
N_EXPERTS = 256
TOP_K = 8
EXPERT_HIDDEN = 256
N_EXPERT_GROUPS = 8
GROUP_SIZE = N_EXPERTS // N_EXPERT_GROUPS
TOPK_GROUPS = 4
ROUTED_SCALE = 2.5
DEPTH = 1
ALPHA = (2.0 * DEPTH) ** 0.25
LN_EPS = 1e-5
GN_EPS = 64e-5

COL_RWKV = 0
COL_SB = 3 * WIDTH
COL_GATE = COL_SB + P_SB
COL_LORA = COL_GATE + P_GATE

LANE = 128
VMEM_LIMIT = 56 * 1024 * 1024


def _cparams(sem):
    return pltpu.CompilerParams(dimension_semantics=sem, vmem_limit_bytes=VMEM_LIMIT)


def _split(x):
    hi = x.astype(BF16)
    lo = (x - hi.astype(F32)).astype(BF16)
    return hi, lo


_NN = (((1,), (0,)), ((), ()))
_NT = (((1,), (1,)), ((), ()))
_TN = (((0,), (0,)), ((), ()))


def _dot(a, b, dims=_NN, passes=1):
    dg = lambda x, y: lax.dot_general(x, y, dims, preferred_element_type=F32)
    if passes == 1:
        return dg(a.astype(BF16), b.astype(BF16))
    ah, al = _split(a)
    bh, bl = _split(b)
    return dg(ah, bh) + (dg(ah, bl) + dg(al, bh))


def _dot_exact_lhs(a_bf16, b, dims=_NN):
    bh, bl = _split(b)
    dg = lambda x, y: lax.dot_general(x, y, dims, preferred_element_type=F32)
    return dg(a_bf16, bh) + dg(a_bf16, bl)


def _softplus(x):
    return jnp.maximum(x, 0.0) + jnp.log1p(jnp.exp(-jnp.abs(x)))


def _sigmoid(x):
    return 1.0 / (1.0 + jnp.exp(-x))


def _layer_norm(x, g, b):
    mu = jnp.mean(x, axis=-1, keepdims=True)
    xc = x - mu
    var = jnp.mean(xc * xc, axis=-1, keepdims=True)
    return xc * lax.rsqrt(var + LN_EPS) * g + b


def _proj_kernel(x_ref, w_ref, o_ref):
    o_ref[...] = jnp.dot(x_ref[...].astype(BF16), w_ref[...], preferred_element_type=F32)


def _project(x, w_bf16, tm, tn):
    n, k = x.shape
    m = w_bf16.shape[1]
    return pl.pallas_call(
        _proj_kernel,
        out_shape=jax.ShapeDtypeStruct((n, m), F32),
        grid=(n // tm, m // tn),
        in_specs=[pl.BlockSpec((tm, k), lambda i, j: (i, 0)), pl.BlockSpec((k, tn), lambda i, j: (0, j))],
        out_specs=pl.BlockSpec((tm, tn), lambda i, j: (i, j)),
        compiler_params=_cparams(("parallel", "parallel")),
        name="in_proj",
    )(x, w_bf16)


def _rwkv_pre_kernel(pr, pk, pv, pl_, qr, qk, qv, ql, valid, mur, muk, muv, mul, w0, wd, a0, wa, wg,
                     r_o, lw_o, k_o, v_o, a_o, g_o):
    ok = valid[...]
    mix = lambda p, q, mu: (p[...] + (q[...] - p[...]) * mu[...])
    r_o[...] = mix(pr, qr, mur) * ok
    k_o[...] = mix(pk, qk, muk) * ok
    v_o[...] = mix(pv, qv, muv) * ok
    lo = mix(pl_, ql, mul)
    w_log = -_softplus(-(w0[...] + _dot(jnp.tanh(lo), wd[...], passes=3))) - 0.5
    lw_o[...] = -jnp.exp(w_log) * ok
    a_o[...] = _sigmoid(a0[...] + _dot(lo, wa[...], passes=3))
    g_o[...] = _dot(_sigmoid(lo), wg[...], passes=3)


def _rwkv_pre(p, p_prev, valid, mu_shift, w0, wd_pad, a0, wa_pad, wg_pad, tm):
    n = p.shape[0]
    row = lambda w, c: pl.BlockSpec((tm, w), lambda i, c=c: (i, c))
    vec = lambda w, c: pl.BlockSpec((1, w), lambda i, c=c: (0, c))
    full = lambda a: pl.BlockSpec(a.shape, lambda i: (0,) * a.ndim)
    rkv = [row(WIDTH, 0), row(WIDTH, 1), row(WIDTH, 2)]
    prev_lora = 3 * WIDTH // LORA
    in_specs = rkv + [row(LORA, COL_LORA // LORA)] + rkv + [row(LORA, prev_lora)] + [
        pl.BlockSpec((tm, 1), lambda i: (i, 0)),
        vec(WIDTH, 0), vec(WIDTH, 1), vec(WIDTH, 2), vec(LORA, prev_lora),
        full(w0), full(wd_pad), full(a0), full(wa_pad), full(wg_pad)]
    out = jax.ShapeDtypeStruct((n, WIDTH), F32)
    return pl.pallas_call(
        _rwkv_pre_kernel,
        out_shape=(out,) * 6,
        grid=(n // tm,),
        in_specs=in_specs,
        out_specs=(pl.BlockSpec((tm, WIDTH), lambda i: (i, 0)),) * 6,
        compiler_params=_cparams(("parallel",)),
        name="rwkv_pre",
    )(p, p, p, p, p_prev, p_prev, p_prev, p_prev, valid, mu_shift, mu_shift, mu_shift, mu_shift,
      w0, wd_pad, a0, wa_pad, wg_pad)


def _wkv_kernel(r_ref, lw_ref, k_ref, v_ref, a_ref, g_ref, s0_ref, kk_ref, ka_ref, rk_ref, gg_ref, gb_ref,
                y_ref, s_ref, *, chunk, heads, passes):
    c = pl.program_id(2)

    @pl.when(c == 0)
    def _():
        s_ref[...] = s0_ref[...]

    ti = lax.broadcasted_iota(jnp.int32, (chunk, chunk), 0)
    tj = lax.broadcasted_iota(jnp.int32, (chunk, chunk), 1)
    incl = ti >= tj
    strict = ti > tj
    tri_incl = jnp.where(incl, 1.0, 0.0).astype(BF16)
    eye = jnp.where(ti == tj, 1.0, 0.0)
    mm = functools.partial(_dot, passes=passes)

    hs = range(heads)
    cols = [slice(h * HEAD_DIM, (h + 1) * HEAD_DIM) for h in hs]
    load = lambda ref: [ref[:, c] for c in cols]
    r, lw, k, v, a, g = (load(ref) for ref in (r_ref, lw_ref, k_ref, v_ref, a_ref, g_ref))
    kk_w, ka_w, rk_w, gg_w, gb_w = (load(ref) for ref in (kk_ref, ka_ref, rk_ref, gg_ref, gb_ref))
    kk = [k[h] * kk_w[h] for h in hs]
    kk = [kk[h] / jnp.maximum(jnp.sqrt(jnp.sum(kk[h] * kk[h], axis=-1, keepdims=True)), 1e-12) for h in hs]
    k2 = [k[h] * (1.0 + (a[h] - 1.0) * ka_w[h]) for h in hs]
    b_vec = [kk[h] * a[h] for h in hs]
    cl = [_dot_exact_lhs(tri_incl, lw[h]) for h in hs]
    cl_end = [cl[h][chunk - 1:chunk, :] for h in hs]
    p_inv = [jnp.exp(-cl[h]) for h in hs]
    p_end = [jnp.exp(cl_end[h] - cl[h]) for h in hs]
    at = [-kk[h] * jnp.exp(cl[h] - lw[h]) for h in hs]
    rt = [r[h] * jnp.exp(cl[h]) for h in hs]
    bt = [b_vec[h] * p_inv[h] for h in hs]
    kt = [k2[h] * p_inv[h] for h in hs]
    l_ab = [jnp.where(strict, mm(at[h], bt[h], _NT), 0.0) for h in hs]
    l_ak = [jnp.where(strict, mm(at[h], kt[h], _NT), 0.0) for h in hs]
    m_rb = [jnp.where(incl, mm(rt[h], bt[h], _NT), 0.0) for h in hs]
    m_rk = [jnp.where(incl, mm(rt[h], kt[h], _NT), 0.0) for h in hs]
    inv = [eye + l_ab[h] for h in hs]
    pw = l_ab
    n = 2
    while n < chunk:
        pw = [mm(pw[h], pw[h]) for h in hs]
        inv = [inv[h] + mm(inv[h], pw[h]) for h in hs]
        n *= 2
    s0 = [s_ref[0, h] for h in hs]
    x = [mm(at[h], s0[h], _NT) + mm(l_ak[h], v[h]) for h in hs]
    u = [mm(inv[h], x[h]) for h in hs]
    y = [mm(rt[h], s0[h], _NT) + mm(m_rb[h], u[h]) + mm(m_rk[h], v[h]) for h in hs]
    s_new = [s0[h] * jnp.exp(cl_end[h]) + mm(u[h], b_vec[h] * p_end[h], _TN) + mm(v[h], k2[h] * p_end[h], _TN)
             for h in hs]
    for h in hs:
        s_ref[0, h] = s_new[h]
        mu = jnp.mean(y[h], axis=-1, keepdims=True)
        yc = y[h] - mu
        var = jnp.mean(yc * yc, axis=-1, keepdims=True)
        yn = yc * lax.rsqrt(var + GN_EPS) * gg_w[h] + gb_w[h]
        bonus = jnp.sum(r[h] * k2[h] * rk_w[h], axis=-1, keepdims=True) * v[h]
        y_ref[:, cols[h]] = (yn + bonus) * g[h]


def _wkv_scan(r, lw, k, v, a, g, s0, k_k, k_a, r_k, gn_g, gn_b, *, row0, seq_stride, n_seq, n_chunks, chunk,
              heads, passes):
    n_rows = n_seq * seq_stride
    hb = N_HEADS // heads
    wblk = heads * HEAD_DIM
    blk0, blk_stride = row0 // chunk, seq_stride // chunk
    row = pl.BlockSpec((chunk, wblk), lambda s, j, c: (blk0 + s * blk_stride + c, j))
    vec = pl.BlockSpec((1, wblk), lambda s, j, c: (0, j))
    st = pl.BlockSpec((1, heads, HEAD_DIM, HEAD_DIM), lambda s, j, c: (s, j, 0, 0))
    y, s_new = pl.pallas_call(
        functools.partial(_wkv_kernel, chunk=chunk, heads=heads, passes=passes),
        out_shape=(jax.ShapeDtypeStruct((n_rows, WIDTH), F32), jax.ShapeDtypeStruct(s0.shape, F32)),
        grid=(n_seq, hb, n_chunks),
        in_specs=[row] * 6 + [st] + [vec] * 5,
        out_specs=(pl.BlockSpec((chunk, wblk), lambda s, j, c: (s * blk_stride + c, j)), st),
        compiler_params=_cparams(("parallel", "parallel", "arbitrary")),
        name="wkv_scan",
    )(r, lw, k, v, a, g, s0, k_k, k_a, r_k, gn_g, gn_b)
    return y, s_new


def _sb_scores(z, vis, tri2):
    sp = jnp.maximum(z, 0.0) + jnp.log(1.0 + jnp.exp(-jnp.abs(z)))
    ls = z - sp
    if vis is not None:
        sp = jnp.where(vis, sp, 0.0)
    later = jnp.dot(jnp.concatenate(_split(sp), axis=1), tri2, preferred_element_type=F32)
    return ls, later, jnp.sum(sp, axis=-1, keepdims=True)


def _sb_weights(ls, later, dead, vis):
    w = jnp.exp(ls - (dead + later))
    return w if vis is None else jnp.where(vis, w, 0.0)


def _later_key_matrix():
    m = lax.broadcasted_iota(jnp.int32, (LANE, LANE), 0)
    j = lax.broadcasted_iota(jnp.int32, (LANE, LANE), 1)
    tri = jnp.where(m > j, 1.0, 0.0).astype(BF16)
    return jnp.concatenate([tri, tri], axis=0)


def _sb_prompt_kernel(bias_ref, q_ref, k_ref, v_ref, o_ref, *, tile):
    hp, qi = pl.program_id(1), pl.program_id(2)
    groups = tile // LANE
    tri2 = _later_key_matrix()
    lane = lax.broadcasted_iota(jnp.int32, (1, LANE), 1)
    row = lax.broadcasted_iota(jnp.int32, (tile, LANE), 0)
    col = lax.broadcasted_iota(jnp.int32, (tile, LANE), 1)
    q = q_ref[...] * (HEAD_DIM ** -0.5)
    qh, bias = [], []
    for h in range(2):
        own = (lane >= h * HEAD_DIM) & (lane < (h + 1) * HEAD_DIM)
        qh.append(jnp.where(own, q, 0.0).astype(BF16))
        bias.append(bias_ref[2 * hp + h])
    units = [(h, g) for h in range(2) for g in range(groups)]

    def block(j, masked, carry):
        base_row = pl.multiple_of(j * tile, tile)
        kb = [k_ref[pl.ds(base_row + g * LANE, LANE), :] for g in range(groups)]
        vb = [v_ref[pl.ds(base_row + g * LANE, LANE), :] for g in range(groups)]
        vis = [row > col + g * LANE for g in range(groups)] if masked else None
        z = {u: lax.dot_general(qh[u[0]], kb[u[1]], _NT, preferred_element_type=F32) + bias[u[0]] for u in units}
        sc = {u: _sb_scores(z[u], vis[u[1]] if masked else None, tri2) for u in units}
        out = []
        for h in range(2):
            dead, acc = carry[h]
            for g in reversed(range(groups)):
                ls, later, total = sc[h, g]
                w = _sb_weights(ls, later, dead, vis[g] if masked else None)
                acc = acc + jnp.dot(w.astype(BF16), vb[g], preferred_element_type=F32)
                dead = dead + total
            out.append((dead, acc))
        return tuple(out)

    zero = (jnp.zeros((tile, 1), F32), jnp.zeros((tile, LANE), F32))
    carry = block(qi, True, (zero, zero))
    carry = lax.fori_loop(0, qi, lambda t, c: block(qi - 1 - t, False, c), carry)
    o_ref[...] = jnp.where(lane < HEAD_DIM, carry[0][1], carry[1][1])


def _sb_prompt(p, kv_bf16, sb_bias, *, n_seq, seq_stride, tile):
    n_q = seq_stride // tile
    qc = COL_SB // LANE
    grid_spec = pltpu.PrefetchScalarGridSpec(
        num_scalar_prefetch=1,
        grid=(n_seq, N_HEADS // 2, n_q),
        in_specs=[pl.BlockSpec((tile, LANE), lambda b, j, i, bias: (b * n_q + i, qc + j)),
                  pl.BlockSpec((seq_stride, LANE), lambda b, j, i, bias: (b, j)),
                  pl.BlockSpec((seq_stride, LANE), lambda b, j, i, bias: (b, WIDTH // LANE + j))],
        out_specs=pl.BlockSpec((tile, LANE), lambda b, j, i, bias: (b * n_q + i, j)),
    )
    return pl.pallas_call(
        functools.partial(_sb_prompt_kernel, tile=tile),
        out_shape=jax.ShapeDtypeStruct((n_seq * seq_stride, WIDTH), F32),
        grid_spec=grid_spec,
        compiler_params=_cparams(("parallel", "parallel", "arbitrary")),
        name="sb_prompt",
    )(sb_bias, p, kv_bf16, kv_bf16)


def _sb_sample_kernel(pt_ref, q_ref, kn_ref, vn_ref, bias_ref, *rest, n_pg, t_new):
    k_pages, v_pages = rest[:n_pg], rest[n_pg:2 * n_pg]
    o_ref, acc_ref, suf_ref = rest[2 * n_pg:]
    s = pl.program_id(1)
    rows = N_HEADS * t_new
    tri2 = _later_key_matrix()
    rh = lax.broadcasted_iota(jnp.int32, (rows, WIDTH), 0) // t_new
    ch = lax.broadcasted_iota(jnp.int32, (rows, WIDTH), 1) // HEAD_DIM
    own = rh == ch
    q = q_ref[...] * (HEAD_DIM ** -0.5)
    q_bd = jnp.where(own, jnp.concatenate([q] * N_HEADS, axis=0), 0.0).astype(BF16)
    bias = bias_ref[...]

    @pl.when(s == 0)
    def _():
        pad = jnp.zeros((PAGE_SIZE - t_new, WIDTH), F32)
        kn = jnp.concatenate([kn_ref[...], pad], axis=0).astype(BF16)
        vn = jnp.concatenate([vn_ref[...], pad], axis=0).astype(BF16)
        key = lax.broadcasted_iota(jnp.int32, (rows, PAGE_SIZE), 1)
        vis = key < lax.broadcasted_iota(jnp.int32, (rows, PAGE_SIZE), 0) % t_new
        z = lax.dot_general(q_bd, kn, _NT, preferred_element_type=F32) + bias
        ls, later, total = _sb_scores(z, vis, tri2)
        w = _sb_weights(ls, later, 0.0, vis)
        acc_ref[...] = jnp.dot(w.astype(BF16), vn, preferred_element_type=F32)
        suf_ref[...] = total

    sc = [_sb_scores(jnp.dot(q_bd, k_pages[i][...].astype(BF16), preferred_element_type=F32) + bias, None, tri2)
          for i in range(n_pg)]
    dead = suf_ref[...]
    acc = acc_ref[...]
    for i in range(n_pg):
        ls, later, total = sc[i]
        w = _sb_weights(ls, later, dead, None)
        acc = acc + lax.dot_general(w.astype(BF16), v_pages[i][...].astype(BF16), _NT, preferred_element_type=F32)
        dead = dead + total
    suf_ref[...] = dead
    acc_ref[...] = acc

    @pl.when(s == pl.num_programs(1) - 1)
    def _():
        acc = jnp.where(own, acc_ref[...], 0.0)
        out = acc[0:t_new]
        for h in range(1, N_HEADS):
            out = out + acc[h * t_new:(h + 1) * t_new]
        o_ref[...] = out


def _sb_sample(q, k_new, v_new, bias_rows, cache_k, cache_v, page_table, *, n_pg):
    n_seq, n_pages = page_table.shape
    t_new = q.shape[0] // n_seq
    rows = N_HEADS * t_new
    tok = pl.BlockSpec((t_new, WIDTH), lambda b, s, pt: (b, 0))

    def page_spec(i):
        return pl.BlockSpec((None, WIDTH, PAGE_SIZE), lambda b, s, pt, i=i: (pt[b, n_pages - 1 - (s * n_pg + i)], 0, 0))

    grid_spec = pltpu.PrefetchScalarGridSpec(
        num_scalar_prefetch=1,
        grid=(n_seq, n_pages // n_pg),
        in_specs=[tok, tok, tok, pl.BlockSpec((rows, 1), lambda b, s, pt: (0, 0))]
        + [page_spec(i) for i in range(n_pg)] * 2,
        out_specs=tok,
        scratch_shapes=[pltpu.VMEM((rows, WIDTH), F32), pltpu.VMEM((rows, 1), F32)],
    )
    return pl.pallas_call(
        functools.partial(_sb_sample_kernel, n_pg=n_pg, t_new=t_new),
        out_shape=jax.ShapeDtypeStruct(q.shape, F32),
        grid_spec=grid_spec,
        compiler_params=_cparams(("parallel", "arbitrary")),
        name="sb_sample",
    )(page_table, q, k_new, v_new, bias_rows, *([cache_k] * n_pg), *([cache_v] * n_pg))


def _merge_kernel(x_ref, ya_ref, yb_ref, ga_ref, gb_ref, wa_ref, wb_ref, wo_ref, g_ref, b_ref, o_ref):
    ma = _sigmoid(ga_ref[...]) * _dot(ya_ref[...], wa_ref[...])
    mb = _sigmoid(gb_ref[...]) * _dot(yb_ref[...], wb_ref[...])
    mixed = _dot(ma + mb, wo_ref[...])
    o_ref[...] = _layer_norm(ALPHA * x_ref[...] + mixed, g_ref[...], b_ref[...])


def _merge(x, y_a, y_b, p, w_proj_a, w_proj_b, w_o, ln_g, ln_b, tm):
    n = x.shape[0]
    row = lambda c: pl.BlockSpec((tm, D_MODEL), lambda i, c=c: (i, c))
    full = lambda a: pl.BlockSpec(a.shape, lambda i: (0,) * a.ndim)
    gate_blk = COL_GATE // D_MODEL
    return pl.pallas_call(
        _merge_kernel,
        out_shape=jax.ShapeDtypeStruct((n, D_MODEL), F32),
        grid=(n // tm,),
        in_specs=[row(0), row(0), row(0), row(gate_blk), row(gate_blk + 1),
                  full(w_proj_a), full(w_proj_b), full(w_o), full(ln_g), full(ln_b)],
        out_specs=row(0),
        compiler_params=_cparams(("parallel",)),
        name="merge_ln1",
    )(x, y_a, y_b, p, p, w_proj_a, w_proj_b, w_o, ln_g, ln_b)


def _router_kernel(x_ref, wr_ref, bias_ref, idx_ref, wgt_ref):
    tm = x_ref.shape[0]
    logits = _dot(wr_ref[...], x_ref[...], _NT, passes=3)
    s = _sigmoid(logits)
    biased = s + bias_ref[...]
    neg = -jnp.inf
    eid = lax.broadcasted_iota(jnp.int32, (N_EXPERTS, tm), 0)
    gid = eid // GROUP_SIZE

    def take_max(cand, ids, n_ids):
        m = jnp.max(cand, axis=0, keepdims=True)
        first = jnp.min(jnp.where(cand == m, ids, n_ids), axis=0, keepdims=True)
        return m, first

    gids = lax.broadcasted_iota(jnp.int32, (N_EXPERT_GROUPS, tm), 0)
    gs = jnp.full((N_EXPERT_GROUPS, tm), neg, F32)
    for gidx in range(N_EXPERT_GROUPS):
        blk = biased[gidx * GROUP_SIZE:(gidx + 1) * GROUP_SIZE]
        ids = lax.broadcasted_iota(jnp.int32, (GROUP_SIZE, tm), 0) + gidx * GROUP_SIZE
        m1, i1 = take_max(blk, ids, N_EXPERTS)
        m2, _ = take_max(jnp.where(ids == i1, neg, blk), ids, N_EXPERTS)
        gs = jnp.where(gids == gidx, m1 + m2, gs)
    cand = jnp.full((N_EXPERTS, tm), neg, F32)
    for _ in range(TOPK_GROUPS):
        _, gi = take_max(gs, gids, N_EXPERT_GROUPS)
        gs = jnp.where(gids == gi, neg, gs)
        cand = jnp.where(gid == gi, biased, cand)
    kids = lax.broadcasted_iota(jnp.int32, (TOP_K, tm), 0)
    idx = jnp.zeros((TOP_K, tm), jnp.int32)
    sel = jnp.zeros((TOP_K, tm), F32)
    for j in range(TOP_K):
        _, ei = take_max(cand, eid, N_EXPERTS)
        hit = eid == ei
        idx = jnp.where(kids == j, ei, idx)
        sel = jnp.where(kids == j, jnp.sum(jnp.where(hit, s, 0.0), axis=0, keepdims=True), sel)
        cand = jnp.where(hit, neg, cand)
    idx_ref[...] = idx
    wgt_ref[...] = sel / jnp.sum(sel, axis=0, keepdims=True) * ROUTED_SCALE


def _router(x, w_router_t, bias_col, tm):
    n = x.shape[0]
    full = lambda a: pl.BlockSpec(a.shape, lambda i: (0,) * a.ndim)
    out = pl.BlockSpec((TOP_K, tm), lambda i: (0, i))
    return pl.pallas_call(
        _router_kernel,
        out_shape=(jax.ShapeDtypeStruct((TOP_K, n), jnp.int32), jax.ShapeDtypeStruct((TOP_K, n), F32)),
        grid=(n // tm,),
        in_specs=[pl.BlockSpec((tm, D_MODEL), lambda i: (i, 0)), full(w_router_t), full(bias_col)],
        out_specs=(out, out),
        compiler_params=_cparams(("parallel",)),
        name="router",
    )(x, w_router_t, bias_col)


def _experts_kernel(be_ref, src_ref, dst_ref, nxt_ref, x_hbm, wt_ref, wg_ref, wu_ref, wd_ref, y_hbm, xin, yout,
                    sem_in, sem_out, *, tb):
    blk = pl.program_id(0)
    n_used = be_ref[be_ref.shape[0] - 1]
    slot = blk % 2

    def gather(ids_ref, buf):
        def body(i, _):
            pltpu.make_async_copy(x_hbm.at[pl.ds(ids_ref[0, i], 1)], xin.at[buf, pl.ds(i, 1)], sem_in.at[buf]).start()
            return 0

        lax.fori_loop(0, tb, body, 0, unroll=8)

    def wait_scatter():
        pltpu.make_async_copy(yout, y_hbm.at[pl.ds(0, tb)], sem_out).wait()

    @pl.when(blk < n_used)
    def _():
        @pl.when(blk == 0)
        def _():
            gather(src_ref, 0)

        pltpu.make_async_copy(x_hbm.at[pl.ds(0, tb)], xin.at[slot], sem_in.at[slot]).wait()

        @pl.when(blk + 1 < n_used)
        def _():
            gather(nxt_ref, 1 - slot)

        xb = xin[slot].astype(BF16)
        hg = jnp.dot(xb, wg_ref[...].astype(BF16), preferred_element_type=F32)
        hu = jnp.dot(xb, wu_ref[...].astype(BF16), preferred_element_type=F32)
        hidden = hg * _sigmoid(hg) * hu
        y = jnp.dot(hidden.astype(BF16), wd_ref[...].astype(BF16), preferred_element_type=F32) * wt_ref[...]

        @pl.when(blk > 0)
        def _():
            wait_scatter()

        yout[...] = y

        def scatter(i, _):
            pltpu.make_async_copy(yout.at[pl.ds(i, 1)], y_hbm.at[pl.ds(dst_ref[0, i], 1)], sem_out).start()
            return 0

        lax.fori_loop(0, tb, scatter, 0, unroll=8)

        @pl.when(blk == n_used - 1)
        def _():
            wait_scatter()


def _experts(x, block_expert, slot_src, slot_dst, slot_w, w_e_gate, w_e_up, w_e_down, *, tb, n_rows_out):
    n_blocks = slot_src.shape[0]
    wspec = lambda a: pl.BlockSpec((None,) + a.shape[1:], lambda i, be: (be[i], 0, 0))
    ids = pl.BlockSpec((None, 1, tb), lambda i, be: (i, 0, 0), memory_space=pltpu.SMEM)
    ids_next = pl.BlockSpec((None, 1, tb), lambda i, be: (jnp.minimum(i + 1, n_blocks - 1), 0, 0),
                            memory_space=pltpu.SMEM)
    grid_spec = pltpu.PrefetchScalarGridSpec(
        num_scalar_prefetch=1,
        grid=(n_blocks,),
        in_specs=[ids, ids, ids_next, pl.BlockSpec(memory_space=pl.ANY),
                  pl.BlockSpec((tb, 1), lambda i, be: (i, 0)),
                  wspec(w_e_gate), wspec(w_e_up), wspec(w_e_down)],
        out_specs=pl.BlockSpec(memory_space=pl.ANY),
        scratch_shapes=[pltpu.VMEM((2, tb, D_MODEL), F32), pltpu.VMEM((tb, D_MODEL), F32),
                        pltpu.SemaphoreType.DMA((2,)), pltpu.SemaphoreType.DMA(())],
    )
    return pl.pallas_call(
        functools.partial(_experts_kernel, tb=tb),
        out_shape=jax.ShapeDtypeStruct((n_rows_out, D_MODEL), F32),
        grid_spec=grid_spec,
        compiler_params=_cparams(("arbitrary",)),
        name="experts",
    )(block_expert, slot_src, slot_dst, slot_src, x, slot_w, w_e_gate, w_e_up, w_e_down)


def _dispatch(top_idx, top_w, n_tok, tb):
    n_assign = TOP_K * n_tok
    n_blocks = -(-n_assign // tb) + N_EXPERTS
    a_idx = jnp.arange(n_assign, dtype=jnp.int32)
    e_sorted, a_sorted, w_sorted = lax.sort((top_idx.reshape(-1), a_idx, top_w.reshape(-1)), num_keys=1)
    experts = jnp.arange(N_EXPERTS, dtype=jnp.int32)
    start = jnp.searchsorted(e_sorted, experts, side="left").astype(jnp.int32)
    end = jnp.searchsorted(e_sorted, experts, side="right").astype(jnp.int32)
    padded = (end - start + tb - 1) // tb * tb
    pad_end = jnp.cumsum(padded)
    pad_start = pad_end - padded
    blk_start = jnp.arange(n_blocks, dtype=jnp.int32) * tb
    block_expert = jnp.minimum(jnp.searchsorted(pad_end, blk_start, side="right"), N_EXPERTS - 1).astype(jnp.int32)
    first = blk_start - pad_start[block_expert] + start[block_expert]
    n_valid = jnp.clip(end[block_expert] - first, 0, tb)
    window = lambda arr: jax.vmap(lambda f: lax.dynamic_slice(jnp.pad(arr, (0, tb)), (f,), (tb,)))(first)
    lane = jnp.arange(tb, dtype=jnp.int32)[None, :]
    valid = lane < n_valid[:, None]
    a_win = window(a_sorted)
    slot_src = jnp.where(valid, a_win % n_tok, 0).reshape(n_blocks, 1, tb)
    slot_dst = jnp.where(valid, a_win, n_assign + lane).reshape(n_blocks, 1, tb)
    slot_w = jnp.where(valid, window(w_sorted), 0.0).reshape(n_blocks * tb, 1)
    n_used = (pad_end[-1] // tb).astype(jnp.int32)
    return jnp.concatenate([block_expert, n_used[None]]), slot_src, slot_dst, slot_w


def _ffn_out_kernel(x_ref, wg_ref, wu_ref, wd_ref, g_ref, b_ref, *rest):
    yr_refs, o_ref = rest[:-1], rest[-1]
    x = x_ref[...]
    xb = x.astype(BF16)
    hg = jnp.dot(xb, wg_ref[...], preferred_element_type=F32)
    hu = jnp.dot(xb, wu_ref[...], preferred_element_type=F32)
    ffn = jnp.dot((hg * _sigmoid(hg) * hu).astype(BF16), wd_ref[...], preferred_element_type=F32)
    for yr_ref in yr_refs:
        ffn = ffn + yr_ref[...]
    o_ref[...] = _layer_norm(ALPHA * x + ffn, g_ref[...], b_ref[...])


def _ffn_out(x, y_routed, w_s_gate, w_s_up, w_s_down, ln_g, ln_b, tm):
    n = x.shape[0]
    full = lambda a: pl.BlockSpec(a.shape, lambda i: (0,) * a.ndim)
    row = pl.BlockSpec((tm, D_MODEL), lambda i: (i, 0))
    slab = lambda j: pl.BlockSpec((tm, D_MODEL), lambda i, j=j: (j * (n // tm) + i, 0))
    return pl.pallas_call(
        _ffn_out_kernel,
        out_shape=jax.ShapeDtypeStruct((n, D_MODEL), F32),
        grid=(n // tm,),
        in_specs=[row, full(w_s_gate), full(w_s_up), full(w_s_down), full(ln_g), full(ln_b)]
        + [slab(j) for j in range(TOP_K)],
        out_specs=row,
        compiler_params=_cparams(("parallel",)),
        name="ffn_out",
    )(x, w_s_gate, w_s_up, w_s_down, ln_g, ln_b, *([y_routed] * TOP_K))


WKV_CHUNK = 64
SB_TILE = 384
SEQ_PAD = 768
ROW_TILE = 512
EXPERT_ROWS = 128
ROUTER_TILE = 256
SAMPLE_PAGES_PER_STEP = 8


def _pages_keys_minor(cache):
    n_pool = cache.shape[0]
    return jnp.transpose(cache, (0, 2, 3, 1)).reshape(n_pool, WIDTH, PAGE_SIZE)


def _shift_rows(p_rwkv, first):
    return jnp.concatenate([first[:, None, :], p_rwkv[:, :-1]], axis=1)


def kernel(x_prompt, x_sample, cache_k, cache_v, page_table, state_wkv, state_shift, meta_tokens, w_in, mu_shift, w0,
           w_decay_up, a0, w_a_up, w_g_up, k_k, k_a, r_k, gn_g, gn_b, sb_bias, w_proj_a, w_proj_b, w_o, ln1_g, ln1_b,
           w_router, router_bias, w_e_gate, w_e_up, w_e_down, w_s_gate, w_s_up, w_s_down, ln2_g, ln2_b):
    assert w_in.shape[0] == DEPTH == 1
    n_p, seq = x_prompt.shape[:2]
    n_s, t_s = x_sample.shape[:2]
    lp = N_META + seq
    lp_pad = -(-lp // SEQ_PAD) * SEQ_PAD
    rows_p = n_p * lp_pad
    rows_s = n_s * t_s
    n_tok = rows_p + rows_s
    assert lp_pad % WKV_CHUNK == 0 and lp_pad % SB_TILE == 0 and n_tok % ROW_TILE == 0 and rows_p % t_s == 0

    meta = jnp.broadcast_to(meta_tokens[None].astype(F32), (n_p, N_META, D_MODEL))
    xp = jnp.concatenate([meta, x_prompt, jnp.zeros((n_p, lp_pad - lp, D_MODEL), F32)], axis=1)
    x = jnp.concatenate([xp.reshape(rows_p, D_MODEL), x_sample.reshape(rows_s, D_MODEL)], axis=0)
    valid = jnp.concatenate([jnp.broadcast_to((jnp.arange(lp_pad) < lp)[None], (n_p, lp_pad)).reshape(rows_p),
                             jnp.ones((rows_s,), jnp.bool_)]).astype(F32)[:, None]

    w = w_in[0]
    w_cols = jnp.concatenate([w[:, :3 * WIDTH], w[:, P_RWKV:], w[:, 3 * WIDTH:P_RWKV]], axis=1)
    p = _project(x, w_cols.astype(BF16), ROW_TILE, 768)
    p_rwkv = jnp.concatenate([p[:, :3 * WIDTH], p[:, COL_LORA:]], axis=1)
    pp3 = p_rwkv[:rows_p].reshape(n_p, lp_pad, P_RWKV)
    ps3 = p_rwkv[rows_p:].reshape(n_s, t_s, P_RWKV)

    p_prev = jnp.concatenate([
        _shift_rows(pp3, jnp.zeros((n_p, P_RWKV), F32)).reshape(rows_p, P_RWKV),
        _shift_rows(ps3, state_shift[0]).reshape(rows_s, P_RWKV)], axis=0)
    zpad = lambda w, lo: jnp.zeros((LORA, WIDTH), F32).at[lo:lo + w.shape[0]].set(w)
    vec = lambda a: a.reshape(1, -1)
    r, lw, k, v, a, g = _rwkv_pre(p, p_prev, valid, mu_shift, vec(w0[0]), zpad(w_decay_up[0], 0), vec(a0[0]),
                                  zpad(w_a_up[0], DECAY_LORA), zpad(w_g_up[0], DECAY_LORA + A_LORA), ROW_TILE)
    head_vecs = (vec(k_k[0]), vec(k_a[0]), vec(r_k[0]), vec(gn_g[0]), vec(gn_b[0]))
    ya_p, wkv_p = _wkv_scan(r, lw, k, v, a, g, jnp.zeros((n_p,) + state_wkv.shape[2:], F32), *head_vecs,
                            row0=0, seq_stride=lp_pad, n_seq=n_p, n_chunks=lp_pad // WKV_CHUNK, chunk=WKV_CHUNK,
                            heads=8, passes=3)
    ya_s, wkv_s = _wkv_scan(r, lw, k, v, a, g, state_wkv[0], *head_vecs,
                            row0=rows_p, seq_stride=t_s, n_seq=n_s, n_chunks=1, chunk=t_s, heads=N_HEADS, passes=3)
    y_a = jnp.concatenate([ya_p, ya_s], axis=0)

    qkv_s = p[rows_p:, COL_SB:COL_GATE].reshape(rows_s, 3, WIDTH)
    yb_p = _sb_prompt(p, p[:, COL_SB + WIDTH:COL_GATE].astype(BF16), sb_bias[0], n_seq=n_p, seq_stride=lp_pad,
                      tile=SB_TILE)
    yb_s = _sb_sample(qkv_s[:, 0], qkv_s[:, 1], qkv_s[:, 2], jnp.repeat(sb_bias[0], t_s)[:, None],
                      _pages_keys_minor(cache_k[0]), _pages_keys_minor(cache_v[0]),
                      page_table, n_pg=SAMPLE_PAGES_PER_STEP)
    y_b = jnp.concatenate([yb_p, yb_s], axis=0)

    x1 = _merge(x, y_a, y_b, p, w_proj_a[0].astype(BF16), w_proj_b[0].astype(BF16), w_o[0].astype(BF16),
                vec(ln1_g[0]), vec(ln1_b[0]), ROW_TILE)

    top_idx, top_w = _router(x1, w_router[0].T, router_bias[0][:, None], ROUTER_TILE)
    block_expert, slot_src, slot_dst, slot_w = _dispatch(top_idx, top_w, n_tok, EXPERT_ROWS)
    y_routed = _experts(x1, block_expert, slot_src, slot_dst, slot_w, w_e_gate[0], w_e_up[0], w_e_down[0],
                        tb=EXPERT_ROWS, n_rows_out=TOP_K * n_tok + EXPERT_ROWS)
    x2 = _ffn_out(x1, y_routed, w_s_gate[0].astype(BF16), w_s_up[0].astype(BF16), w_s_down[0].astype(BF16),
                  vec(ln2_g[0]), vec(ln2_b[0]), ROW_TILE)

    x2p = x2[:rows_p].reshape(n_p, lp_pad, D_MODEL)
    qkv_p = p[:rows_p, COL_SB:COL_GATE].reshape(n_p, lp_pad, 3, N_HEADS, HEAD_DIM)[:, :lp]
    qkv_s5 = qkv_s.reshape(n_s, t_s, 3, N_HEADS, HEAD_DIM)
    return (x2p[:, N_META:lp], x2[rows_p:].reshape(n_s, t_s, D_MODEL),
            qkv_p[None, :, :, 1], qkv_p[None, :, :, 2], qkv_s5[None, :, :, 1], qkv_s5[None, :, :, 2],
            wkv_p[None], pp3[:, lp - 1][None], wkv_s[None], ps3[:, t_s - 1][None])
```

```python
import functools

import jax
import jax.numpy as jnp
from jax import lax
from jax.experimental import pallas as pl
from jax.experimental.pallas import tpu as pltpu

F32 = jnp.float32
BF16 = jnp.bfloat16

D_MODEL = 1024
N_META = 16
HEAD_DIM = 64
N_HEADS = 16
WIDTH = N_HEADS * HEAD_DIM
DECAY_LORA = 64
A_LORA = 64
GATE_LORA = 128
LORA = DECAY_LORA + A_LORA + GATE_LORA
P_RWKV = 3 * WIDTH + LORA
P_SB = 3 * WIDTH
P_GATE = 2 * D_MODEL
P_IN = P_RWKV + P_SB + P_GATE
PAGE_SIZE = 128
N_EXPERTS = 256
TOP_K = 8
EXPERT_HIDDEN = 256
N_EXPERT_GROUPS = 8
GROUP_SIZE = N_EXPERTS // N_EXPERT_GROUPS
TOPK_GROUPS = 4
ROUTED_SCALE = 2.5
DEPTH = 1
ALPHA = (2.0 * DEPTH) ** 0.25
LN_EPS = 1e-5
GN_EPS = 64e-5

COL_RWKV = 0
COL_SB = 3 * WIDTH
COL_GATE = COL_SB + P_SB
COL_LORA = COL_GATE + P_GATE

LANE = 128
VMEM_LIMIT = 56 * 1024 * 1024


def _cparams(sem):
    return pltpu.CompilerParams(dimension_semantics=sem, vmem_limit_bytes=VMEM_LIMIT)


def _split(x):
    hi = x.astype(BF16)
    lo = (x - hi.astype(F32)).astype(BF16)
    return hi, lo


_NN = (((1,), (0,)), ((), ()))
_NT = (((1,), (1,)), ((), ()))
_TN = (((0,), (0,)), ((), ()))


def _dot(a, b, dims=_NN, passes=1):
    dg = lambda x, y: lax.dot_general(x, y, dims, preferred_element_type=F32)
    if passes == 1:
        return dg(a.astype(BF16), b.astype(BF16))
    ah, al = _split(a)
    bh, bl = _split(b)
    return dg(ah, bh) + (dg(ah, bl) + dg(al, bh))


def _dot_exact_lhs(a_bf16, b, dims=_NN):
    bh, bl = _split(b)
    dg = lambda x, y: lax.dot_general(x, y, dims, preferred_element_type=F32)
    return dg(a_bf16, bh) + dg(a_bf16, bl)


def _softplus(x):
    return jnp.maximum(x, 0.0) + jnp.log1p(jnp.exp(-jnp.abs(x)))


def _sigmoid(x):
    return 1.0 / (1.0 + jnp.exp(-x))


def _layer_norm(x, g, b):
    mu = jnp.mean(x, axis=-1, keepdims=True)
    xc = x - mu
    var = jnp.mean(xc * xc, axis=-1, keepdims=True)
    return xc * lax.rsqrt(var + LN_EPS) * g + b


def _proj_kernel(x_ref, w_ref, o_ref):
    o_ref[...] = jnp.dot(x_ref[...].astype(BF16), w_ref[...], preferred_element_type=F32)


def _project(x, w_bf16, tm, tn):
    n, k = x.shape
    m = w_bf16.shape[1]
    return pl.pallas_call(
        _proj_kernel,
        out_shape=jax.ShapeDtypeStruct((n, m), F32),
        grid=(n // tm, m // tn),
        in_specs=[pl.BlockSpec((tm, k), lambda i, j: (i, 0)), pl.BlockSpec((k, tn), lambda i, j: (0, j))],
        out_specs=pl.BlockSpec((tm, tn), lambda i, j: (i, j)),
        compiler_params=_cparams(("parallel", "parallel")),
        name="in_proj",
    )(x, w_bf16)


def _rwkv_pre_kernel(pr, pk, pv, pl_, qr, qk, qv, ql, valid, mur, muk, muv, mul, w0, wd, a0, wa, wg,
                     r_o, lw_o, k_o, v_o, a_o, g_o):
    ok = valid[...]
    mix = lambda p, q, mu: (p[...] + (q[...] - p[...]) * mu[...])
    r_o[...] = mix(pr, qr, mur) * ok
    k_o[...] = mix(pk, qk, muk) * ok
    v_o[...] = mix(pv, qv, muv) * ok
    lo = mix(pl_, ql, mul)
    w_log = -_softplus(-(w0[...] + _dot(jnp.tanh(lo), wd[...], passes=3))) - 0.5
    lw_o[...] = -jnp.exp(w_log) * ok
    a_o[...] = _sigmoid(a0[...] + _dot(lo, wa[...], passes=3))
    g_o[...] = _dot(_sigmoid(lo), wg[...], passes=3)


def _rwkv_pre(p, p_prev, valid, mu_shift, w0, wd_pad, a0, wa_pad, wg_pad, tm):
    n = p.shape[0]
    row = lambda w, c: pl.BlockSpec((tm, w), lambda i, c=c: (i, c))
    vec = lambda w, c: pl.BlockSpec((1, w), lambda i, c=c: (0, c))
    full = lambda a: pl.BlockSpec(a.shape, lambda i: (0,) * a.ndim)
    rkv = [row(WIDTH, 0), row(WIDTH, 1), row(WIDTH, 2)]
    prev_lora = 3 * WIDTH // LORA
    in_specs = rkv + [row(LORA, COL_LORA // LORA)] + rkv + [row(LORA, prev_lora)] + [
        pl.BlockSpec((tm, 1), lambda i: (i, 0)),
        vec(WIDTH, 0), vec(WIDTH, 1), vec(WIDTH, 2), vec(LORA, prev_lora),
        full(w0), full(wd_pad), full(a0), full(wa_pad), full(wg_pad)]
    out = jax.ShapeDtypeStruct((n, WIDTH), F32)
    return pl.pallas_call(
        _rwkv_pre_kernel,
        out_shape=(out,) * 6,
        grid=(n // tm,),
        in_specs=in_specs,
        out_specs=(pl.BlockSpec((tm, WIDTH), lambda i: (i, 0)),) * 6,
        compiler_params=_cparams(("parallel",)),
        name="rwkv_pre",
    )(p, p, p, p, p_prev, p_prev, p_prev, p_prev, valid, mu_shift, mu_shift, mu_shift, mu_shift,
      w0, wd_pad, a0, wa_pad, wg_pad)


def _wkv_kernel(r_ref, lw_ref, k_ref, v_ref, a_ref, g_ref, s0_ref, kk_ref, ka_ref, rk_ref, gg_ref, gb_ref,
                y_ref, s_ref, *, chunk, heads, passes):
    c = pl.program_id(2)

    @pl.when(c == 0)
    def _():
        s_ref[...] = s0_ref[...]

    ti = lax.broadcasted_iota(jnp.int32, (chunk, chunk), 0)
    tj = lax.broadcasted_iota(jnp.int32, (chunk, chunk), 1)
    incl = ti >= tj
    strict = ti > tj
    tri_incl = jnp.where(incl, 1.0, 0.0).astype(BF16)
    eye = jnp.where(ti == tj, 1.0, 0.0)
    mm = functools.partial(_dot, passes=passes)

    hs = range(heads)
    cols = [slice(h * HEAD_DIM, (h + 1) * HEAD_DIM) for h in hs]
    load = lambda ref: [ref[:, c] for c in cols]
    r, lw, k, v, a, g = (load(ref) for ref in (r_ref, lw_ref, k_ref, v_ref, a_ref, g_ref))
    kk_w, ka_w, rk_w, gg_w, gb_w = (load(ref) for ref in (kk_ref, ka_ref, rk_ref, gg_ref, gb_ref))
    kk = [k[h] * kk_w[h] for h in hs]
    kk = [kk[h] / jnp.maximum(jnp.sqrt(jnp.sum(kk[h] * kk[h], axis=-1, keepdims=True)), 1e-12) for h in hs]
    k2 = [k[h] * (1.0 + (a[h] - 1.0) * ka_w[h]) for h in hs]
    b_vec = [kk[h] * a[h] for h in hs]
    cl = [_dot_exact_lhs(tri_incl, lw[h]) for h in hs]
    cl_end = [cl[h][chunk - 1:chunk, :] for h in hs]
    p_inv = [jnp.exp(-cl[h]) for h in hs]
    p_end = [jnp.exp(cl_end[h] - cl[h]) for h in hs]
    at = [-kk[h] * jnp.exp(cl[h] - lw[h]) for h in hs]
    rt = [r[h] * jnp.exp(cl[h]) for h in hs]
    bt = [b_vec[h] * p_inv[h] for h in hs]
    kt = [k2[h] * p_inv[h] for h in hs]
    l_ab = [jnp.where(strict, mm(at[h], bt[h], _NT), 0.0) for h in hs]
    l_ak = [jnp.where(strict, mm(at[h], kt[h], _NT), 0.0) for h in hs]
    m_rb = [jnp.where(incl, mm(rt[h], bt[h], _NT), 0.0) for h in hs]
    m_rk = [jnp.where(incl, mm(rt[h], kt[h], _NT), 0.0) for h in hs]
    inv = [eye + l_ab[h] for h in hs]
    pw = l_ab
    n = 2
    while n < chunk:
        pw = [mm(pw[h], pw[h]) for h in hs]
        inv = [inv[h] + mm(inv[h], pw[h]) for h in hs]
        n *= 2
    s0 = [s_ref[0, h] for h in hs]
    x = [mm(at[h], s0[h], _NT) + mm(l_ak[h], v[h]) for h in hs]
    u = [mm(inv[h], x[h]) for h in hs]
    y = [mm(rt[h], s0[h], _NT) + mm(m_rb[h], u[h]) + mm(m_rk[h], v[h]) for h in hs]
    s_new = [s0[h] * jnp.exp(cl_end[h]) + mm(u[h], b_vec[h] * p_end[h], _TN) + mm(v[h], k2[h] * p_end[h], _TN)
             for h in hs]
    for h in hs:
        s_ref[0, h] = s_new[h]
        mu = jnp.mean(y[h], axis=-1, keepdims=True)
        yc = y[h] - mu
        var = jnp.mean(yc * yc, axis=-1, keepdims=True)
        yn = yc * lax.rsqrt(var + GN_EPS) * gg_w[h] + gb_w[h]
        bonus = jnp.sum(r[h] * k2[h] * rk_w[h], axis=-1, keepdims=True) * v[h]
        y_ref[:, cols[h]] = (yn + bonus) * g[h]


def _wkv_scan(r, lw, k, v, a, g, s0, k_k, k_a, r_k, gn_g, gn_b, *, row0, seq_stride, n_seq, n_chunks, chunk,
              heads, passes):
    n_rows = n_seq * seq_stride
    hb = N_HEADS // heads
    wblk = heads * HEAD_DIM
    blk0, blk_stride = row0 // chunk, seq_stride // chunk
    row = pl.BlockSpec((chunk, wblk), lambda s, j, c: (blk0 + s * blk_stride + c, j))
    vec = pl.BlockSpec((1, wblk), lambda s, j, c: (0, j))
    st = pl.BlockSpec((1, heads, HEAD_DIM, HEAD_DIM), lambda s, j, c: (s, j, 0, 0))
    y, s_new = pl.pallas_call(
        functools.partial(_wkv_kernel, chunk=chunk, heads=heads, passes=passes),
        out_shape=(jax.ShapeDtypeStruct((n_rows, WIDTH), F32), jax.ShapeDtypeStruct(s0.shape, F32)),
        grid=(n_seq, hb, n_chunks),
        in_specs=[row] * 6 + [st] + [vec] * 5,
        out_specs=(pl.BlockSpec((chunk, wblk), lambda s, j, c: (s * blk_stride + c, j)), st),
        compiler_params=_cparams(("parallel", "parallel", "arbitrary")),
        name="wkv_scan",
    )(r, lw, k, v, a, g, s0, k_k, k_a, r_k, gn_g, gn_b)
    return y, s_new


def _sb_scores(z, vis, tri2):
    sp = jnp.maximum(z, 0.0) + jnp.log(1.0 + jnp.exp(-jnp.abs(z)))
    ls = z - sp
    if vis is not None:
        sp = jnp.where(vis, sp, 0.0)
    later = jnp.dot(jnp.concatenate(_split(sp), axis=1), tri2, preferred_element_type=F32)
    return ls, later, jnp.sum(sp, axis=-1, keepdims=True)


def _sb_weights(ls, later, dead, vis):
    w = jnp.exp(ls - (dead + later))
    return w if vis is None else jnp.where(vis, w, 0.0)


def _later_key_matrix():
    m = lax.broadcasted_iota(jnp.int32, (LANE, LANE), 0)
    j = lax.broadcasted_iota(jnp.int32, (LANE, LANE), 1)
    tri = jnp.where(m > j, 1.0, 0.0).astype(BF16)
    return jnp.concatenate([tri, tri], axis=0)


def _sb_prompt_kernel(bias_ref, q_ref, k_ref, v_ref, o_ref, *, tile, pairs):
    hp, qi = pl.program_id(1), pl.program_id(2)
    groups = tile // LANE
    tri2 = _later_key_matrix()
    lane = lax.broadcasted_iota(jnp.int32, (1, LANE), 1)
    row = lax.broadcasted_iota(jnp.int32, (tile, LANE), 0)
    col = lax.broadcasted_iota(jnp.int32, (tile, LANE), 1)
    n_h = 2 * pairs
    qh, bias = [], []
    for h in range(n_h):
        q = q_ref[:, (h // 2) * LANE:(h // 2 + 1) * LANE] * (HEAD_DIM ** -0.5)
        own = (lane >= (h % 2) * HEAD_DIM) & (lane < (h % 2 + 1) * HEAD_DIM)
        qh.append(jnp.where(own, q, 0.0).astype(BF16))
        bias.append(bias_ref[n_h * hp + h])
    units = [(h, g) for h in range(n_h) for g in range(groups)]

    def block(j, masked, carry):
        base_row = pl.multiple_of(j * tile, tile)
        kb = {(c, g): k_ref[pl.ds(base_row + g * LANE, LANE), c * LANE:(c + 1) * LANE]
              for c in range(pairs) for g in range(groups)}
        vblk = [v_ref[pl.ds(base_row, tile), c * LANE:(c + 1) * LANE] for c in range(pairs)]
        vis = [row > col + g * LANE for g in range(groups)] if masked else None
        z = {u: lax.dot_general(qh[u[0]], kb[u[0] // 2, u[1]], _NT, preferred_element_type=F32) + bias[u[0]]
             for u in units}
        sc = {u: _sb_scores(z[u], vis[u[1]] if masked else None, tri2) for u in units}
        out = []
        for h in range(n_h):
            dead, acc = carry[h]
            w = [None] * groups
            for g in reversed(range(groups)):
                ls, later, total = sc[h, g]
                w[g] = _sb_weights(ls, later, dead, vis[g] if masked else None).astype(BF16)
                dead = dead + total
            acc = acc + jnp.dot(jnp.concatenate(w, axis=1), vblk[h // 2], preferred_element_type=F32)
            out.append((dead, acc))
        return tuple(out)

    zero = (jnp.zeros((tile, 1), F32), jnp.zeros((tile, LANE), F32))
    carry = block(qi, True, (zero,) * n_h)
    carry = lax.fori_loop(0, qi, lambda t, c: block(qi - 1 - t, False, c), carry)
    for c in range(pairs):
        o_ref[:, c * LANE:(c + 1) * LANE] = jnp.where(lane < HEAD_DIM, carry[2 * c][1], carry[2 * c + 1][1])


def _sb_prompt(p, kv_bf16, sb_bias, *, n_seq, seq_stride, tile, pairs):
    n_q = seq_stride // tile
    wide = pairs * LANE
    qc = COL_SB // wide
    grid_spec = pltpu.PrefetchScalarGridSpec(
        num_scalar_prefetch=1,
        grid=(n_seq, WIDTH // wide, n_q),
        in_specs=[pl.BlockSpec((tile, wide), lambda b, j, i, bias: (b * n_q + i, qc + j)),
                  pl.BlockSpec((seq_stride, wide), lambda b, j, i, bias: (b, j)),
                  pl.BlockSpec((seq_stride, wide), lambda b, j, i, bias: (b, WIDTH // wide + j))],
        out_specs=pl.BlockSpec((tile, wide), lambda b, j, i, bias: (b * n_q + i, j)),
    )
    return pl.pallas_call(
        functools.partial(_sb_prompt_kernel, tile=tile, pairs=pairs),
        out_shape=jax.ShapeDtypeStruct((n_seq * seq_stride, WIDTH), F32),
        grid_spec=grid_spec,
        compiler_params=_cparams(("parallel", "parallel", "arbitrary")),
        name="sb_prompt",
    )(sb_bias, p, kv_bf16, kv_bf16)


def _sb_sample_kernel(pt_ref, q_ref, kn_ref, vn_ref, bias_ref, *rest, n_pg, t_new):
    k_pages, v_pages = rest[:n_pg], rest[n_pg:2 * n_pg]
    o_ref, acc_ref, suf_ref = rest[2 * n_pg:]
    s = pl.program_id(1)
    rows = N_HEADS * t_new
    tri2 = _later_key_matrix()
    rh = lax.broadcasted_iota(jnp.int32, (rows, WIDTH), 0) // t_new
    ch = lax.broadcasted_iota(jnp.int32, (rows, WIDTH), 1) // HEAD_DIM
    own = rh == ch
    q = q_ref[...] * (HEAD_DIM ** -0.5)
    q_bd = jnp.where(own, jnp.concatenate([q] * N_HEADS, axis=0), 0.0).astype(BF16)
    bias = bias_ref[...]

    @pl.when(s == 0)
    def _():
        pad = jnp.zeros((PAGE_SIZE - t_new, WIDTH), F32)
        kn = jnp.concatenate([kn_ref[...], pad], axis=0).astype(BF16)
        vn = jnp.concatenate([vn_ref[...], pad], axis=0).astype(BF16)
        key = lax.broadcasted_iota(jnp.int32, (rows, PAGE_SIZE), 1)
        vis = key < lax.broadcasted_iota(jnp.int32, (rows, PAGE_SIZE), 0) % t_new
        z = lax.dot_general(q_bd, kn, _NT, preferred_element_type=F32) + bias
        ls, later, total = _sb_scores(z, vis, tri2)
        w = _sb_weights(ls, later, 0.0, vis)
        acc_ref[...] = jnp.dot(w.astype(BF16), vn, preferred_element_type=F32)
        suf_ref[...] = total

    sc = [_sb_scores(jnp.dot(q_bd, k_pages[i][...].astype(BF16), preferred_element_type=F32) + bias, None, tri2)
          for i in range(n_pg)]
    dead = suf_ref[...]
    acc = acc_ref[...]
    for i in range(n_pg):
        ls, later, total = sc[i]
        w = _sb_weights(ls, later, dead, None)
        acc = acc + lax.dot_general(w.astype(BF16), v_pages[i][...].astype(BF16), _NT, preferred_element_type=F32)
        dead = dead + total
    suf_ref[...] = dead
    acc_ref[...] = acc

    @pl.when(s == pl.num_programs(1) - 1)
    def _():
        acc = jnp.where(own, acc_ref[...], 0.0)
        out = acc[0:t_new]
        for h in range(1, N_HEADS):
            out = out + acc[h * t_new:(h + 1) * t_new]
        o_ref[...] = out


def _sb_sample(q, k_new, v_new, bias_rows, cache_k, cache_v, page_table, *, n_pg):
    n_seq, n_pages = page_table.shape
    t_new = q.shape[0] // n_seq
    rows = N_HEADS * t_new
    tok = pl.BlockSpec((t_new, WIDTH), lambda b, s, pt: (b, 0))

    def page_spec(i):
        return pl.BlockSpec((None, WIDTH, PAGE_SIZE), lambda b, s, pt, i=i: (pt[b, n_pages - 1 - (s * n_pg + i)], 0, 0))

    grid_spec = pltpu.PrefetchScalarGridSpec(
        num_scalar_prefetch=1,
        grid=(n_seq, n_pages // n_pg),
        in_specs=[tok, tok, tok, pl.BlockSpec((rows, 1), lambda b, s, pt: (0, 0))]
        + [page_spec(i) for i in range(n_pg)] * 2,
        out_specs=tok,
        scratch_shapes=[pltpu.VMEM((rows, WIDTH), F32), pltpu.VMEM((rows, 1), F32)],
    )
    return pl.pallas_call(
        functools.partial(_sb_sample_kernel, n_pg=n_pg, t_new=t_new),
        out_shape=jax.ShapeDtypeStruct(q.shape, F32),
        grid_spec=grid_spec,
        compiler_params=_cparams(("parallel", "arbitrary")),
        name="sb_sample",
    )(page_table, q, k_new, v_new, bias_rows, *([cache_k] * n_pg), *([cache_v] * n_pg))


def _merge_kernel(x_ref, ya_ref, yb_ref, ga_ref, gb_ref, wa_ref, wb_ref, wo_ref, g_ref, b_ref, o_ref):
    ma = _sigmoid(ga_ref[...]) * _dot(ya_ref[...], wa_ref[...])
    mb = _sigmoid(gb_ref[...]) * _dot(yb_ref[...], wb_ref[...])
    mixed = _dot(ma + mb, wo_ref[...])
    o_ref[...] = _layer_norm(ALPHA * x_ref[...] + mixed, g_ref[...], b_ref[...])


def _merge(x, y_a, y_b, p, w_proj_a, w_proj_b, w_o, ln_g, ln_b, tm):
    n = x.shape[0]
    row = lambda c: pl.BlockSpec((tm, D_MODEL), lambda i, c=c: (i, c))
    full = lambda a: pl.BlockSpec(a.shape, lambda i: (0,) * a.ndim)
    gate_blk = COL_GATE // D_MODEL
    return pl.pallas_call(
        _merge_kernel,
        out_shape=jax.ShapeDtypeStruct((n, D_MODEL), F32),
        grid=(n // tm,),
        in_specs=[row(0), row(0), row(0), row(gate_blk), row(gate_blk + 1),
                  full(w_proj_a), full(w_proj_b), full(w_o), full(ln_g), full(ln_b)],
        out_specs=row(0),
        compiler_params=_cparams(("parallel",)),
        name="merge_ln1",
    )(x, y_a, y_b, p, p, w_proj_a, w_proj_b, w_o, ln_g, ln_b)


def _router_kernel(x_ref, wr_ref, bias_ref, idx_ref, wgt_ref):
    tm = x_ref.shape[0]
    logits = _dot(wr_ref[...], x_ref[...], _NT, passes=3)
    s = _sigmoid(logits)
    biased = s + bias_ref[...]
    neg = -jnp.inf
    eid = lax.broadcasted_iota(jnp.int32, (N_EXPERTS, tm), 0)
    gid = eid // GROUP_SIZE

    def take_max(cand, ids, n_ids):
        m = jnp.max(cand, axis=0, keepdims=True)
        first = jnp.min(jnp.where(cand == m, ids, n_ids), axis=0, keepdims=True)
        return m, first

    gids = lax.broadcasted_iota(jnp.int32, (N_EXPERT_GROUPS, tm), 0)
    gs = jnp.full((N_EXPERT_GROUPS, tm), neg, F32)
    for gidx in range(N_EXPERT_GROUPS):
        blk = biased[gidx * GROUP_SIZE:(gidx + 1) * GROUP_SIZE]
        ids = lax.broadcasted_iota(jnp.int32, (GROUP_SIZE, tm), 0) + gidx * GROUP_SIZE
        m1, i1 = take_max(blk, ids, N_EXPERTS)
        m2, _ = take_max(jnp.where(ids == i1, neg, blk), ids, N_EXPERTS)
        gs = jnp.where(gids == gidx, m1 + m2, gs)
    cand = jnp.full((N_EXPERTS, tm), neg, F32)
    for _ in range(TOPK_GROUPS):
        _, gi = take_max(gs, gids, N_EXPERT_GROUPS)
        gs = jnp.where(gids == gi, neg, gs)
        cand = jnp.where(gid == gi, biased, cand)
    kids = lax.broadcasted_iota(jnp.int32, (TOP_K, tm), 0)
    idx = jnp.zeros((TOP_K, tm), jnp.int32)
    sel = jnp.zeros((TOP_K, tm), F32)
    for j in range(TOP_K):
        _, ei = take_max(cand, eid, N_EXPERTS)
        hit = eid == ei
        idx = jnp.where(kids == j, ei, idx)
        sel = jnp.where(kids == j, jnp.sum(jnp.where(hit, s, 0.0), axis=0, keepdims=True), sel)
        cand = jnp.where(hit, neg, cand)
    idx_ref[...] = idx
    wgt_ref[...] = sel / jnp.sum(sel, axis=0, keepdims=True) * ROUTED_SCALE


def _router(x, w_router_t, bias_col, tm):
    n = x.shape[0]
    full = lambda a: pl.BlockSpec(a.shape, lambda i: (0,) * a.ndim)
    out = pl.BlockSpec((TOP_K, tm), lambda i: (0, i))
    return pl.pallas_call(
        _router_kernel,
        out_shape=(jax.ShapeDtypeStruct((TOP_K, n), jnp.int32), jax.ShapeDtypeStruct((TOP_K, n), F32)),
        grid=(n // tm,),
        in_specs=[pl.BlockSpec((tm, D_MODEL), lambda i: (i, 0)), full(w_router_t), full(bias_col)],
        out_specs=(out, out),
        compiler_params=_cparams(("parallel",)),
        name="router",
    )(x, w_router_t, bias_col)


def _experts_kernel(pe_ref, pb_ref, lo_ref, hi_ref, tok_ref, dst_ref, nxt_ref, x_hbm, wt_ref, wg_ref, wu_ref, wd_ref,
                    y_hbm, xin, yout, sem_in, sem_out, *, tb, n_assign):
    pair = pl.program_id(0)
    n_used = pe_ref[pe_ref.shape[0] - 1]
    slot = pair % 2

    def gather(ids_ref, buf):
        def body(i, _):
            pltpu.make_async_copy(x_hbm.at[pl.ds(ids_ref[0, i], 1)], xin.at[buf, pl.ds(i, 1)], sem_in.at[buf]).start()
            return 0

        lax.fori_loop(0, tb, body, 0, unroll=8)

    def wait_scatter():
        pltpu.make_async_copy(yout, y_hbm.at[pl.ds(0, tb)], sem_out).wait()

    @pl.when(pair < n_used)
    def _():
        @pl.when(pair == 0)
        def _():
            gather(tok_ref, 0)

        pltpu.make_async_copy(x_hbm.at[pl.ds(0, tb)], xin.at[slot], sem_in.at[slot]).wait()

        @pl.when(pair + 1 < n_used)
        def _():
            gather(nxt_ref, 1 - slot)

        xb = xin[slot].astype(BF16)
        hg = jnp.dot(xb, wg_ref[...].astype(BF16), preferred_element_type=F32)
        hu = jnp.dot(xb, wu_ref[...].astype(BF16), preferred_element_type=F32)
        hidden = hg * _sigmoid(hg) * hu
        y = jnp.dot(hidden.astype(BF16), wd_ref[...].astype(BF16), preferred_element_type=F32) * wt_ref[...]

        @pl.when(pair > 0)
        def _():
            wait_scatter()

        yout[...] = y
        lo, hi = lo_ref[pair], hi_ref[pair]

        def scatter(i, _):
            row = jnp.where((i >= lo) & (i < hi), dst_ref[0, i], n_assign + i)
            pltpu.make_async_copy(yout.at[pl.ds(i, 1)], y_hbm.at[pl.ds(row, 1)], sem_out).start()
            return 0

        lax.fori_loop(0, tb, scatter, 0, unroll=8)

        @pl.when(pair == n_used - 1)
        def _():
            wait_scatter()


def _experts(x, pairs, tok_sorted, a_sorted, w_sorted, w_e_gate, w_e_up, w_e_down, *, tb):
    pair_expert, pair_block, lo, hi = pairs
    n_pairs = pair_block.shape[0]
    n_assign = a_sorted.shape[0] * tb
    wspec = lambda a: pl.BlockSpec((None,) + a.shape[1:], lambda i, pe, pb, lo, hi: (pe[i], 0, 0))
    ids = pl.BlockSpec((None, 1, tb), lambda i, pe, pb, lo, hi: (pb[i], 0, 0), memory_space=pltpu.SMEM)
    ids_next = pl.BlockSpec((None, 1, tb), lambda i, pe, pb, lo, hi: (pb[jnp.minimum(i + 1, n_pairs - 1)], 0, 0),
                            memory_space=pltpu.SMEM)
    grid_spec = pltpu.PrefetchScalarGridSpec(
        num_scalar_prefetch=4,
        grid=(n_pairs,),
        in_specs=[ids, ids, ids_next, pl.BlockSpec(memory_space=pl.ANY),
                  pl.BlockSpec((tb, 1), lambda i, pe, pb, lo, hi: (pb[i], 0)),
                  wspec(w_e_gate), wspec(w_e_up), wspec(w_e_down)],
        out_specs=pl.BlockSpec(memory_space=pl.ANY),
        scratch_shapes=[pltpu.VMEM((2, tb, D_MODEL), F32), pltpu.VMEM((tb, D_MODEL), F32),
                        pltpu.SemaphoreType.DMA((2,)), pltpu.SemaphoreType.DMA(())],
    )
    return pl.pallas_call(
        functools.partial(_experts_kernel, tb=tb, n_assign=n_assign),
        out_shape=jax.ShapeDtypeStruct((n_assign + tb, D_MODEL), F32),
        grid_spec=grid_spec,
        compiler_params=_cparams(("arbitrary",)),
        name="experts",
    )(pair_expert, pair_block, lo, hi, tok_sorted, a_sorted, tok_sorted, x, w_sorted, w_e_gate, w_e_up, w_e_down)


def _dispatch(top_idx, top_w, n_tok, tb):
    n_assign = TOP_K * n_tok
    n_blocks = n_assign // tb
    n_pairs = n_blocks + N_EXPERTS
    a_idx = jnp.arange(n_assign, dtype=jnp.int32)
    e_sorted, a_sorted, w_sorted = lax.sort((top_idx.reshape(-1), a_idx, top_w.reshape(-1)), num_keys=1)
    experts = jnp.arange(N_EXPERTS, dtype=jnp.int32)
    start = jnp.searchsorted(e_sorted, experts, side="left").astype(jnp.int32)
    end = jnp.searchsorted(e_sorted, experts, side="right").astype(jnp.int32)
    visits = jnp.where(end > start, (end - 1) // tb - start // tb + 1, 0)
    visit_end = jnp.cumsum(visits)
    pair = jnp.arange(n_pairs, dtype=jnp.int32)
    pair_expert = jnp.minimum(jnp.searchsorted(visit_end, pair, side="right"), N_EXPERTS - 1).astype(jnp.int32)
    pair_block = start[pair_expert] // tb + pair - (visit_end - visits)[pair_expert]
    pair_block = jnp.clip(pair_block, 0, n_blocks - 1).astype(jnp.int32)
    lo = jnp.clip(start[pair_expert] - pair_block * tb, 0, tb).astype(jnp.int32)
    hi = jnp.clip(end[pair_expert] - pair_block * tb, 0, tb).astype(jnp.int32)
    n_used = visit_end[-1].astype(jnp.int32)
    pairs = (jnp.concatenate([pair_expert, n_used[None]]), pair_block, lo, hi)
    blocks = lambda v: v.reshape(n_blocks, 1, tb)
    return pairs, blocks(a_sorted % n_tok), blocks(a_sorted), w_sorted.reshape(n_assign, 1)


def _ffn_out_kernel(x_ref, wg_ref, wu_ref, wd_ref, g_ref, b_ref, *rest):
    yr_refs, o_ref = rest[:-1], rest[-1]
    x = x_ref[...]
    xb = x.astype(BF16)
    hg = jnp.dot(xb, wg_ref[...], preferred_element_type=F32)
    hu = jnp.dot(xb, wu_ref[...], preferred_element_type=F32)
    ffn = jnp.dot((hg * _sigmoid(hg) * hu).astype(BF16), wd_ref[...], preferred_element_type=F32)
    for yr_ref in yr_refs:
        ffn = ffn + yr_ref[...]
    o_ref[...] = _layer_norm(ALPHA * x + ffn, g_ref[...], b_ref[...])


def _ffn_out(x, y_routed, w_s_gate, w_s_up, w_s_down, ln_g, ln_b, tm):
    n = x.shape[0]
    full = lambda a: pl.BlockSpec(a.shape, lambda i: (0,) * a.ndim)
    row = pl.BlockSpec((tm, D_MODEL), lambda i: (i, 0))
    slab = lambda j: pl.BlockSpec((tm, D_MODEL), lambda i, j=j: (j * (n // tm) + i, 0))
    return pl.pallas_call(
        _ffn_out_kernel,
        out_shape=jax.ShapeDtypeStruct((n, D_MODEL), F32),
        grid=(n // tm,),
        in_specs=[row, full(w_s_gate), full(w_s_up), full(w_s_down), full(ln_g), full(ln_b)]
        + [slab(j) for j in range(TOP_K)],
        out_specs=row,
        compiler_params=_cparams(("parallel",)),
        name="ffn_out",
    )(x, w_s_gate, w_s_up, w_s_down, ln_g, ln_b, *([y_routed] * TOP_K))


WKV_CHUNK = 64
SB_TILE = 384
SB_HEAD_PAIRS = 2
SEQ_PAD = 768
ROW_TILE = 512
EXPERT_ROWS = 128
ROUTER_TILE = 256
SAMPLE_PAGES_PER_STEP = 8


def _pages_keys_minor(cache):
    n_pool = cache.shape[0]
    return jnp.transpose(cache, (0, 2, 3, 1)).reshape(n_pool, WIDTH, PAGE_SIZE)


def _shift_rows(p_rwkv, first):
    return jnp.concatenate([first[:, None, :], p_rwkv[:, :-1]], axis=1)


def kernel(x_prompt, x_sample, cache_k, cache_v, page_table, state_wkv, state_shift, meta_tokens, w_in, mu_shift, w0,
           w_decay_up, a0, w_a_up, w_g_up, k_k, k_a, r_k, gn_g, gn_b, sb_bias, w_proj_a, w_proj_b, w_o, ln1_g, ln1_b,
           w_router, router_bias, w_e_gate, w_e_up, w_e_down, w_s_gate, w_s_up, w_s_down, ln2_g, ln2_b):
    assert w_in.shape[0] == DEPTH == 1
    n_p, seq = x_prompt.shape[:2]
    n_s, t_s = x_sample.shape[:2]
    lp = N_META + seq
    lp_pad = -(-lp // SEQ_PAD) * SEQ_PAD
    rows_p = n_p * lp_pad
    rows_s = n_s * t_s
    n_tok = rows_p + rows_s
    assert lp_pad % WKV_CHUNK == 0 and lp_pad % SB_TILE == 0 and n_tok % ROW_TILE == 0 and rows_p % t_s == 0

    meta = jnp.broadcast_to(meta_tokens[None].astype(F32), (n_p, N_META, D_MODEL))
    xp = jnp.concatenate([meta, x_prompt, jnp.zeros((n_p, lp_pad - lp, D_MODEL), F32)], axis=1)
    x = jnp.concatenate([xp.reshape(rows_p, D_MODEL), x_sample.reshape(rows_s, D_MODEL)], axis=0)
    valid = jnp.concatenate([jnp.broadcast_to((jnp.arange(lp_pad) < lp)[None], (n_p, lp_pad)).reshape(rows_p),
                             jnp.ones((rows_s,), jnp.bool_)]).astype(F32)[:, None]

    w = w_in[0]
    w_cols = jnp.concatenate([w[:, :3 * WIDTH], w[:, P_RWKV:], w[:, 3 * WIDTH:P_RWKV]], axis=1)
    p = _project(x, w_cols.astype(BF16), ROW_TILE, 768)
    p_rwkv = jnp.concatenate([p[:, :3 * WIDTH], p[:, COL_LORA:]], axis=1)
    pp3 = p_rwkv[:rows_p].reshape(n_p, lp_pad, P_RWKV)
    ps3 = p_rwkv[rows_p:].reshape(n_s, t_s, P_RWKV)

    p_prev = jnp.concatenate([
        _shift_rows(pp3, jnp.zeros((n_p, P_RWKV), F32)).reshape(rows_p, P_RWKV),
        _shift_rows(ps3, state_shift[0]).reshape(rows_s, P_RWKV)], axis=0)
    zpad = lambda w, lo: jnp.zeros((LORA, WIDTH), F32).at[lo:lo + w.shape[0]].set(w)
    vec = lambda a: a.reshape(1, -1)
    r, lw, k, v, a, g = _rwkv_pre(p, p_prev, valid, mu_shift, vec(w0[0]), zpad(w_decay_up[0], 0), vec(a0[0]),
                                  zpad(w_a_up[0], DECAY_LORA), zpad(w_g_up[0], DECAY_LORA + A_LORA), ROW_TILE)
    head_vecs = (vec(k_k[0]), vec(k_a[0]), vec(r_k[0]), vec(gn_g[0]), vec(gn_b[0]))
    ya_p, wkv_p = _wkv_scan(r, lw, k, v, a, g, jnp.zeros((n_p,) + state_wkv.shape[2:], F32), *head_vecs,
                            row0=0, seq_stride=lp_pad, n_seq=n_p, n_chunks=lp_pad // WKV_CHUNK, chunk=WKV_CHUNK,
                            heads=8, passes=3)
    ya_s, wkv_s = _wkv_scan(r, lw, k, v, a, g, state_wkv[0], *head_vecs,
                            row0=rows_p, seq_stride=t_s, n_seq=n_s, n_chunks=1, chunk=t_s, heads=N_HEADS, passes=3)
    y_a = jnp.concatenate([ya_p, ya_s], axis=0)

    qkv_s = p[rows_p:, COL_SB:COL_GATE].reshape(rows_s, 3, WIDTH)
    yb_p = _sb_prompt(p, p[:, COL_SB + WIDTH:COL_GATE].astype(BF16), sb_bias[0], n_seq=n_p, seq_stride=lp_pad,
                      tile=SB_TILE, pairs=SB_HEAD_PAIRS)
    yb_s = _sb_sample(qkv_s[:, 0], qkv_s[:, 1], qkv_s[:, 2], jnp.repeat(sb_bias[0], t_s)[:, None],
                      _pages_keys_minor(cache_k[0]), _pages_keys_minor(cache_v[0]),
                      page_table, n_pg=SAMPLE_PAGES_PER_STEP)
    y_b = jnp.concatenate([yb_p, yb_s], axis=0)

    x1 = _merge(x, y_a, y_b, p, w_proj_a[0].astype(BF16), w_proj_b[0].astype(BF16), w_o[0].astype(BF16),
                vec(ln1_g[0]), vec(ln1_b[0]), ROW_TILE)

    top_idx, top_w = _router(x1, w_router[0].T, router_bias[0][:, None], ROUTER_TILE)
    pairs, tok_sorted, a_sorted, w_sorted = _dispatch(top_idx, top_w, n_tok, EXPERT_ROWS)
    y_routed = _experts(x1, pairs, tok_sorted, a_sorted, w_sorted, w_e_gate[0], w_e_up[0], w_e_down[0], tb=EXPERT_ROWS)
    x2 = _ffn_out(x1, y_routed, w_s_gate[0].astype(BF16), w_s_up[0].astype(BF16), w_s_down[0].astype(BF16),
                  vec(ln2_g[0]), vec(ln2_b[0]), ROW_TILE)

    x2p = x2[:rows_p].reshape(n_p, lp_pad, D_MODEL)
    kv_p = p[:rows_p, COL_SB + WIDTH:COL_GATE].reshape(n_p, lp_pad, 2, N_HEADS, HEAD_DIM)[:, :lp]
    qkv_s5 = qkv_s.reshape(n_s, t_s, 3, N_HEADS, HEAD_DIM)
    return (x2p[:, N_META:lp], x2[rows_p:].reshape(n_s, t_s, D_MODEL),
            kv_p[None, :, :, 0], kv_p[None, :, :, 1], qkv_s5[None, :, :, 1], qkv_s5[None, :, :, 2],
            wkv_p[None], pp3[:, lp - 1][None], wkv_s[None], ps3[:, t_s - 1][None])
```

```python
import functools

import jax
import jax.numpy as jnp
from jax import lax
from jax.experimental import pallas as pl
from jax.experimental.pallas import tpu as pltpu

F32 = jnp.float32
BF16 = jnp.bfloat16

D_MODEL = 1024
N_META = 16
HEAD_DIM = 64
N_HEADS = 16
WIDTH = N_HEADS * HEAD_DIM
DECAY_LORA = 64
A_LORA = 64
GATE_LORA = 128
LORA = DECAY_LORA + A_LORA + GATE_LORA
P_RWKV = 3 * WIDTH + LORA
P_SB = 3 * WIDTH
P_GATE = 2 * D_MODEL
P_IN = P_RWKV + P_SB + P_GATE
PAGE_SIZE = 128
N_EXPERTS = 256
TOP_K = 8
EXPERT_HIDDEN = 256
N_EXPERT_GROUPS = 8
GROUP_SIZE = N_EXPERTS // N_EXPERT_GROUPS
TOPK_GROUPS = 4
ROUTED_SCALE = 2.5
DEPTH = 1
ALPHA = (2.0 * DEPTH) ** 0.25
LN_EPS = 1e-5
GN_EPS = 64e-5

COL_RWKV = 0
COL_SB = 3 * WIDTH
COL_GATE = COL_SB + P_SB
COL_LORA = COL_GATE + P_GATE

LANE = 128
VMEM_LIMIT = 56 * 1024 * 1024


def _cparams(sem):
    return pltpu.CompilerParams(dimension_semantics=sem, vmem_limit_bytes=VMEM_LIMIT)


def _split(x):
    hi = x.astype(BF16)
    lo = (x - hi.astype(F32)).astype(BF16)
    return hi, lo


_NN = (((1,), (0,)), ((), ()))
_NT = (((1,), (1,)), ((), ()))
_TN = (((0,), (0,)), ((), ()))


def _dot(a, b, dims=_NN, passes=1):
    dg = lambda x, y: lax.dot_general(x, y, dims, preferred_element_type=F32)
    if passes == 1:
        return dg(a.astype(BF16), b.astype(BF16))
    ah, al = _split(a)
    bh, bl = _split(b)
    return dg(ah, bh) + (dg(ah, bl) + dg(al, bh))


def _dot_exact_lhs(a_bf16, b, dims=_NN):
    bh, bl = _split(b)
    dg = lambda x, y: lax.dot_general(x, y, dims, preferred_element_type=F32)
    return dg(a_bf16, bh) + dg(a_bf16, bl)


def _softplus(x):
    return jnp.maximum(x, 0.0) + jnp.log1p(jnp.exp(-jnp.abs(x)))


def _sigmoid(x):
    return 1.0 / (1.0 + jnp.exp(-x))


def _layer_norm(x, g, b):
    mu = jnp.mean(x, axis=-1, keepdims=True)
    xc = x - mu
    var = jnp.mean(xc * xc, axis=-1, keepdims=True)
    return xc * lax.rsqrt(var + LN_EPS) * g + b


def _proj_kernel(x_ref, w_ref, o_ref):
    o_ref[...] = jnp.dot(x_ref[...].astype(BF16), w_ref[...], preferred_element_type=F32)


def _project(x, w_bf16, tm, tn):
    n, k = x.shape
    m = w_bf16.shape[1]
    return pl.pallas_call(
        _proj_kernel,
        out_shape=jax.ShapeDtypeStruct((n, m), F32),
        grid=(n // tm, m // tn),
        in_specs=[pl.BlockSpec((tm, k), lambda i, j: (i, 0)), pl.BlockSpec((k, tn), lambda i, j: (0, j))],
        out_specs=pl.BlockSpec((tm, tn), lambda i, j: (i, j)),
        compiler_params=_cparams(("parallel", "parallel")),
        name="in_proj",
    )(x, w_bf16)


def _rwkv_pre_kernel(pr, pk, pv, pl_, qr, qk, qv, ql, valid, mur, muk, muv, mul, w0, wd, a0, wa, wg,
                     r_o, lw_o, k_o, v_o, a_o, g_o):
    ok = valid[...]
    mix = lambda p, q, mu: (p[...] + (q[...] - p[...]) * mu[...])
    r_o[...] = mix(pr, qr, mur) * ok
    k_o[...] = mix(pk, qk, muk) * ok
    v_o[...] = mix(pv, qv, muv) * ok
    lo = mix(pl_, ql, mul)
    w_log = -_softplus(-(w0[...] + _dot(jnp.tanh(lo), wd[...], passes=3))) - 0.5
    lw_o[...] = -jnp.exp(w_log) * ok
    a_o[...] = _sigmoid(a0[...] + _dot(lo, wa[...], passes=3))
    g_o[...] = _dot(_sigmoid(lo), wg[...], passes=3)


def _rwkv_pre(p, p_prev, valid, mu_shift, w0, wd_pad, a0, wa_pad, wg_pad, tm):
    n = p.shape[0]
    row = lambda w, c: pl.BlockSpec((tm, w), lambda i, c=c: (i, c))
    vec = lambda w, c: pl.BlockSpec((1, w), lambda i, c=c: (0, c))
    full = lambda a: pl.BlockSpec(a.shape, lambda i: (0,) * a.ndim)
    rkv = [row(WIDTH, 0), row(WIDTH, 1), row(WIDTH, 2)]
    prev_lora = 3 * WIDTH // LORA
    in_specs = rkv + [row(LORA, COL_LORA // LORA)] + rkv + [row(LORA, prev_lora)] + [
        pl.BlockSpec((tm, 1), lambda i: (i, 0)),
        vec(WIDTH, 0), vec(WIDTH, 1), vec(WIDTH, 2), vec(LORA, prev_lora),
        full(w0), full(wd_pad), full(a0), full(wa_pad), full(wg_pad)]
    out = jax.ShapeDtypeStruct((n, WIDTH), F32)
    return pl.pallas_call(
        _rwkv_pre_kernel,
        out_shape=(out,) * 6,
        grid=(n // tm,),
        in_specs=in_specs,
        out_specs=(pl.BlockSpec((tm, WIDTH), lambda i: (i, 0)),) * 6,
        compiler_params=_cparams(("parallel",)),
        name="rwkv_pre",
    )(p, p, p, p, p_prev, p_prev, p_prev, p_prev, valid, mu_shift, mu_shift, mu_shift, mu_shift,
      w0, wd_pad, a0, wa_pad, wg_pad)


def _wkv_kernel(r_ref, lw_ref, k_ref, v_ref, a_ref, g_ref, s0_ref, kk_ref, ka_ref, rk_ref, gg_ref, gb_ref,
                y_ref, s_ref, *, chunk, heads, passes):
    c = pl.program_id(2)

    @pl.when(c == 0)
    def _():
        s_ref[...] = s0_ref[...]

    ti = lax.broadcasted_iota(jnp.int32, (chunk, chunk), 0)
    tj = lax.broadcasted_iota(jnp.int32, (chunk, chunk), 1)
    incl = ti >= tj
    strict = ti > tj
    tri_incl = jnp.where(incl, 1.0, 0.0).astype(BF16)
    eye = jnp.where(ti == tj, 1.0, 0.0)
    mm = functools.partial(_dot, passes=passes)

    hs = range(heads)
    cols = [slice(h * HEAD_DIM, (h + 1) * HEAD_DIM) for h in hs]
    load = lambda ref: [ref[:, c] for c in cols]
    r, lw, k, v, a, g = (load(ref) for ref in (r_ref, lw_ref, k_ref, v_ref, a_ref, g_ref))
    kk_w, ka_w, rk_w, gg_w, gb_w = (load(ref) for ref in (kk_ref, ka_ref, rk_ref, gg_ref, gb_ref))
    kk = [k[h] * kk_w[h] for h in hs]
    kk = [kk[h] / jnp.maximum(jnp.sqrt(jnp.sum(kk[h] * kk[h], axis=-1, keepdims=True)), 1e-12) for h in hs]
    k2 = [k[h] * (1.0 + (a[h] - 1.0) * ka_w[h]) for h in hs]
    b_vec = [kk[h] * a[h] for h in hs]
    cl = [_dot_exact_lhs(tri_incl, lw[h]) for h in hs]
    cl_end = [cl[h][chunk - 1:chunk, :] for h in hs]
    p_inv = [jnp.exp(-cl[h]) for h in hs]
    p_end = [jnp.exp(cl_end[h] - cl[h]) for h in hs]
    at = [-kk[h] * jnp.exp(cl[h] - lw[h]) for h in hs]
    rt = [r[h] * jnp.exp(cl[h]) for h in hs]
    bt = [b_vec[h] * p_inv[h] for h in hs]
    kt = [k2[h] * p_inv[h] for h in hs]
    l_ab = [jnp.where(strict, mm(at[h], bt[h], _NT), 0.0) for h in hs]
    l_ak = [jnp.where(strict, mm(at[h], kt[h], _NT), 0.0) for h in hs]
    m_rb = [jnp.where(incl, _dot(rt[h], bt[h], _NT), 0.0) for h in hs]
    m_rk = [jnp.where(incl, _dot(rt[h], kt[h], _NT), 0.0) for h in hs]
    inv = [eye + l_ab[h] for h in hs]
    pw = l_ab
    n = 2
    while n < chunk:
        pw = [mm(pw[h], pw[h]) for h in hs]
        inv = [inv[h] + mm(inv[h], pw[h]) for h in hs]
        n *= 2
    s0 = [s_ref[0, h] for h in hs]
    x = [mm(at[h], s0[h], _NT) + mm(l_ak[h], v[h]) for h in hs]
    u = [mm(inv[h], x[h]) for h in hs]
    y = [_dot(rt[h], s0[h], _NT) + _dot(m_rb[h], u[h]) + _dot(m_rk[h], v[h]) for h in hs]
    s_new = [s0[h] * jnp.exp(cl_end[h]) + mm(u[h], b_vec[h] * p_end[h], _TN) + mm(v[h], k2[h] * p_end[h], _TN)
             for h in hs]
    for h in hs:
        s_ref[0, h] = s_new[h]
        mu = jnp.mean(y[h], axis=-1, keepdims=True)
        yc = y[h] - mu
        var = jnp.mean(yc * yc, axis=-1, keepdims=True)
        yn = yc * lax.rsqrt(var + GN_EPS) * gg_w[h] + gb_w[h]
        bonus = jnp.sum(r[h] * k2[h] * rk_w[h], axis=-1, keepdims=True) * v[h]
        y_ref[:, cols[h]] = (yn + bonus) * g[h]


def _wkv_scan(r, lw, k, v, a, g, s0, k_k, k_a, r_k, gn_g, gn_b, *, row0, seq_stride, n_seq, n_chunks, chunk,
              heads, passes):
    n_rows = n_seq * seq_stride
    hb = N_HEADS // heads
    wblk = heads * HEAD_DIM
    blk0, blk_stride = row0 // chunk, seq_stride // chunk
    row = pl.BlockSpec((chunk, wblk), lambda s, j, c: (blk0 + s * blk_stride + c, j))
    vec = pl.BlockSpec((1, wblk), lambda s, j, c: (0, j))
    st = pl.BlockSpec((1, heads, HEAD_DIM, HEAD_DIM), lambda s, j, c: (s, j, 0, 0))
    y, s_new = pl.pallas_call(
        functools.partial(_wkv_kernel, chunk=chunk, heads=heads, passes=passes),
        out_shape=(jax.ShapeDtypeStruct((n_rows, WIDTH), F32), jax.ShapeDtypeStruct(s0.shape, F32)),
        grid=(n_seq, hb, n_chunks),
        in_specs=[row] * 6 + [st] + [vec] * 5,
        out_specs=(pl.BlockSpec((chunk, wblk), lambda s, j, c: (s * blk_stride + c, j)), st),
        compiler_params=_cparams(("parallel", "parallel", "arbitrary")),
        name="wkv_scan",
    )(r, lw, k, v, a, g, s0, k_k, k_a, r_k, gn_g, gn_b)
    return y, s_new


def _sb_scores(z, vis, tri2):
    sp = jnp.maximum(z, 0.0) + jnp.log(1.0 + jnp.exp(-jnp.abs(z)))
    ls = z - sp
    if vis is not None:
        sp = jnp.where(vis, sp, 0.0)
    later = jnp.dot(jnp.concatenate(_split(sp), axis=1), tri2, preferred_element_type=F32)
    return ls, later, jnp.sum(sp, axis=-1, keepdims=True)


def _sb_weights(ls, later, dead, vis):
    w = jnp.exp(ls - (dead + later))
    return w if vis is None else jnp.where(vis, w, 0.0)


DEAD_MAX = 150.0


def _later_key_matrix():
    m = lax.broadcasted_iota(jnp.int32, (LANE, LANE), 0)
    j = lax.broadcasted_iota(jnp.int32, (LANE, LANE), 1)
    tri = jnp.where(m > j, 1.0, 0.0).astype(BF16)
    return jnp.concatenate([tri, tri], axis=0)


def _sb_prompt_kernel(bias_ref, q_ref, k_ref, v_ref, o_ref, *, tile, pairs):
    hp, qi = pl.program_id(1), pl.program_id(2)
    groups = tile // LANE
    tri2 = _later_key_matrix()
    lane = lax.broadcasted_iota(jnp.int32, (1, LANE), 1)
    row = lax.broadcasted_iota(jnp.int32, (tile, LANE), 0)
    col = lax.broadcasted_iota(jnp.int32, (tile, LANE), 1)
    n_h = 2 * pairs
    qh, bias = [], []
    for h in range(n_h):
        q = q_ref[:, (h // 2) * LANE:(h // 2 + 1) * LANE] * (HEAD_DIM ** -0.5)
        own = (lane >= (h % 2) * HEAD_DIM) & (lane < (h % 2 + 1) * HEAD_DIM)
        qh.append(jnp.where(own, q, 0.0).astype(BF16))
        bias.append(bias_ref[n_h * hp + h])
    units = [(h, g) for h in range(n_h) for g in range(groups)]

    def block(j, masked, carry):
        base_row = pl.multiple_of(j * tile, tile)
        kb = {(c, g): k_ref[pl.ds(base_row + g * LANE, LANE), c * LANE:(c + 1) * LANE]
              for c in range(pairs) for g in range(groups)}
        vblk = [v_ref[pl.ds(base_row, tile), c * LANE:(c + 1) * LANE] for c in range(pairs)]
        vis = [row > col + g * LANE for g in range(groups)] if masked else None
        z = {u: lax.dot_general(qh[u[0]], kb[u[0] // 2, u[1]], _NT, preferred_element_type=F32) + bias[u[0]]
             for u in units}
        sc = {u: _sb_scores(z[u], vis[u[1]] if masked else None, tri2) for u in units}
        out = []
        for h in range(n_h):
            dead, acc = carry[h]
            w = [None] * groups
            for g in reversed(range(groups)):
                ls, later, total = sc[h, g]
                w[g] = _sb_weights(ls, later, dead, vis[g] if masked else None).astype(BF16)
                dead = dead + total
            acc = acc + jnp.dot(jnp.concatenate(w, axis=1), vblk[h // 2], preferred_element_type=F32)
            out.append((dead, acc))
        return tuple(out)

    def alive(carry):
        least = carry[0][0]
        for dead, _ in carry[1:]:
            least = jnp.minimum(least, dead)
        return jnp.min(least) < DEAD_MAX

    zero = (jnp.zeros((tile, 1), F32), jnp.zeros((tile, LANE), F32))
    carry = block(qi, True, (zero,) * n_h)
    _, _, carry = lax.while_loop(
        lambda s: (s[0] < qi) & s[1],
        lambda s: (lambda c: (s[0] + 1, alive(c), c))(block(qi - 1 - s[0], False, s[2])),
        (0, alive(carry), carry))
    for c in range(pairs):
        o_ref[:, c * LANE:(c + 1) * LANE] = jnp.where(lane < HEAD_DIM, carry[2 * c][1], carry[2 * c + 1][1])


def _sb_prompt(p, kv_bf16, sb_bias, *, n_seq, seq_stride, tile, pairs):
    n_q = seq_stride // tile
    wide = pairs * LANE
    qc = COL_SB // wide
    grid_spec = pltpu.PrefetchScalarGridSpec(
        num_scalar_prefetch=1,
        grid=(n_seq, WIDTH // wide, n_q),
        in_specs=[pl.BlockSpec((tile, wide), lambda b, j, i, bias: (b * n_q + i, qc + j)),
                  pl.BlockSpec((seq_stride, wide), lambda b, j, i, bias: (b, j)),
                  pl.BlockSpec((seq_stride, wide), lambda b, j, i, bias: (b, WIDTH // wide + j))],
        out_specs=pl.BlockSpec((tile, wide), lambda b, j, i, bias: (b * n_q + i, j)),
    )
    return pl.pallas_call(
        functools.partial(_sb_prompt_kernel, tile=tile, pairs=pairs),
        out_shape=jax.ShapeDtypeStruct((n_seq * seq_stride, WIDTH), F32),
        grid_spec=grid_spec,
        compiler_params=_cparams(("parallel", "parallel", "arbitrary")),
        name="sb_prompt",
    )(sb_bias, p, kv_bf16, kv_bf16)


def _sb_sample_kernel(pt_ref, q_ref, kn_ref, vn_ref, bias_ref, *rest, n_pg, t_new):
    k_pages, v_pages = rest[:n_pg], rest[n_pg:2 * n_pg]
    o_ref, acc_ref, suf_ref = rest[2 * n_pg:]
    s = pl.program_id(1)
    rows = N_HEADS * t_new
    tri2 = _later_key_matrix()
    rh = lax.broadcasted_iota(jnp.int32, (rows, WIDTH), 0) // t_new
    ch = lax.broadcasted_iota(jnp.int32, (rows, WIDTH), 1) // HEAD_DIM
    own = rh == ch
    q = q_ref[...] * (HEAD_DIM ** -0.5)
    q_bd = jnp.where(own, jnp.concatenate([q] * N_HEADS, axis=0), 0.0).astype(BF16)
    bias = bias_ref[...]

    @pl.when(s == 0)
    def _():
        pad = jnp.zeros((PAGE_SIZE - t_new, WIDTH), F32)
        kn = jnp.concatenate([kn_ref[...], pad], axis=0).astype(BF16)
        vn = jnp.concatenate([vn_ref[...], pad], axis=0).astype(BF16)
        key = lax.broadcasted_iota(jnp.int32, (rows, PAGE_SIZE), 1)
        vis = key < lax.broadcasted_iota(jnp.int32, (rows, PAGE_SIZE), 0) % t_new
        z = lax.dot_general(q_bd, kn, _NT, preferred_element_type=F32) + bias
        ls, later, total = _sb_scores(z, vis, tri2)
        w = _sb_weights(ls, later, 0.0, vis)
        acc_ref[...] = jnp.dot(w.astype(BF16), vn, preferred_element_type=F32)
        suf_ref[...] = total

    sc = [_sb_scores(jnp.dot(q_bd, k_pages[i][...].astype(BF16), preferred_element_type=F32) + bias, None, tri2)
          for i in range(n_pg)]
    dead = suf_ref[...]
    acc = acc_ref[...]
    for i in range(n_pg):
        ls, later, total = sc[i]
        w = _sb_weights(ls, later, dead, None)
        acc = acc + lax.dot_general(w.astype(BF16), v_pages[i][...].astype(BF16), _NT, preferred_element_type=F32)
        dead = dead + total
    suf_ref[...] = dead
    acc_ref[...] = acc

    @pl.when(s == pl.num_programs(1) - 1)
    def _():
        acc = jnp.where(own, acc_ref[...], 0.0)
        out = acc[0:t_new]
        for h in range(1, N_HEADS):
            out = out + acc[h * t_new:(h + 1) * t_new]
        o_ref[...] = out


def _sb_sample(q, k_new, v_new, bias_rows, cache_k, cache_v, page_table, *, n_pg):
    n_seq, n_pages = page_table.shape
    t_new = q.shape[0] // n_seq
    rows = N_HEADS * t_new
    tok = pl.BlockSpec((t_new, WIDTH), lambda b, s, pt: (b, 0))

    def page_spec(i):
        return pl.BlockSpec((None, WIDTH, PAGE_SIZE), lambda b, s, pt, i=i: (pt[b, n_pages - 1 - (s * n_pg + i)], 0, 0))

    grid_spec = pltpu.PrefetchScalarGridSpec(
        num_scalar_prefetch=1,
        grid=(n_seq, n_pages // n_pg),
        in_specs=[tok, tok, tok, pl.BlockSpec((rows, 1), lambda b, s, pt: (0, 0))]
        + [page_spec(i) for i in range(n_pg)] * 2,
        out_specs=tok,
        scratch_shapes=[pltpu.VMEM((rows, WIDTH), F32), pltpu.VMEM((rows, 1), F32)],
    )
    return pl.pallas_call(
        functools.partial(_sb_sample_kernel, n_pg=n_pg, t_new=t_new),
        out_shape=jax.ShapeDtypeStruct(q.shape, F32),
        grid_spec=grid_spec,
        compiler_params=_cparams(("parallel", "arbitrary")),
        name="sb_sample",
    )(page_table, q, k_new, v_new, bias_rows, *([cache_k] * n_pg), *([cache_v] * n_pg))


def _merge_kernel(x_ref, ya_ref, yb_ref, ga_ref, gb_ref, wa_ref, wb_ref, wo_ref, g_ref, b_ref, o_ref):
    ma = _sigmoid(ga_ref[...]) * _dot(ya_ref[...], wa_ref[...])
    mb = _sigmoid(gb_ref[...]) * _dot(yb_ref[...], wb_ref[...])
    mixed = _dot(ma + mb, wo_ref[...])
    o_ref[...] = _layer_norm(ALPHA * x_ref[...] + mixed, g_ref[...], b_ref[...])


def _merge(x, y_a, y_b, p, w_proj_a, w_proj_b, w_o, ln_g, ln_b, tm):
    n = x.shape[0]
    row = lambda c: pl.BlockSpec((tm, D_MODEL), lambda i, c=c: (i, c))
    full = lambda a: pl.BlockSpec(a.shape, lambda i: (0,) * a.ndim)
    gate_blk = COL_GATE // D_MODEL
    return pl.pallas_call(
        _merge_kernel,
        out_shape=jax.ShapeDtypeStruct((n, D_MODEL), F32),
        grid=(n // tm,),
        in_specs=[row(0), row(0), row(0), row(gate_blk), row(gate_blk + 1),
                  full(w_proj_a), full(w_proj_b), full(w_o), full(ln_g), full(ln_b)],
        out_specs=row(0),
        compiler_params=_cparams(("parallel",)),
        name="merge_ln1",
    )(x, y_a, y_b, p, p, w_proj_a, w_proj_b, w_o, ln_g, ln_b)


def _router_kernel(x_ref, wr_ref, bias_ref, idx_ref, wgt_ref):
    tm = x_ref.shape[0]
    logits = _dot(wr_ref[...], x_ref[...], _NT, passes=3)
    s = _sigmoid(logits)
    biased = s + bias_ref[...]
    neg = -jnp.inf
    eid = lax.broadcasted_iota(jnp.int32, (N_EXPERTS, tm), 0)
    gid = eid // GROUP_SIZE

    def take_max(cand, ids, n_ids):
        m = jnp.max(cand, axis=0, keepdims=True)
        first = jnp.min(jnp.where(cand == m, ids, n_ids), axis=0, keepdims=True)
        return m, first

    gids = lax.broadcasted_iota(jnp.int32, (N_EXPERT_GROUPS, tm), 0)
    gs = jnp.full((N_EXPERT_GROUPS, tm), neg, F32)
    for gidx in range(N_EXPERT_GROUPS):
        blk = biased[gidx * GROUP_SIZE:(gidx + 1) * GROUP_SIZE]
        ids = lax.broadcasted_iota(jnp.int32, (GROUP_SIZE, tm), 0) + gidx * GROUP_SIZE
        m1, i1 = take_max(blk, ids, N_EXPERTS)
        m2, _ = take_max(jnp.where(ids == i1, neg, blk), ids, N_EXPERTS)
        gs = jnp.where(gids == gidx, m1 + m2, gs)
    cand = jnp.full((N_EXPERTS, tm), neg, F32)
    for _ in range(TOPK_GROUPS):
        _, gi = take_max(gs, gids, N_EXPERT_GROUPS)
        gs = jnp.where(gids == gi, neg, gs)
        cand = jnp.where(gid == gi, biased, cand)
    kids = lax.broadcasted_iota(jnp.int32, (TOP_K, tm), 0)
    idx = jnp.zeros((TOP_K, tm), jnp.int32)
    sel = jnp.zeros((TOP_K, tm), F32)
    for j in range(TOP_K):
        _, ei = take_max(cand, eid, N_EXPERTS)
        hit = eid == ei
        idx = jnp.where(kids == j, ei, idx)
        sel = jnp.where(kids == j, jnp.sum(jnp.where(hit, s, 0.0), axis=0, keepdims=True), sel)
        cand = jnp.where(hit, neg, cand)
    idx_ref[...] = idx
    wgt_ref[...] = sel / jnp.sum(sel, axis=0, keepdims=True) * ROUTED_SCALE


def _router(x, w_router_t, bias_col, tm):
    n = x.shape[0]
    full = lambda a: pl.BlockSpec(a.shape, lambda i: (0,) * a.ndim)
    out = pl.BlockSpec((TOP_K, tm), lambda i: (0, i))
    return pl.pallas_call(
        _router_kernel,
        out_shape=(jax.ShapeDtypeStruct((TOP_K, n), jnp.int32), jax.ShapeDtypeStruct((TOP_K, n), F32)),
        grid=(n // tm,),
        in_specs=[pl.BlockSpec((tm, D_MODEL), lambda i: (i, 0)), full(w_router_t), full(bias_col)],
        out_specs=(out, out),
        compiler_params=_cparams(("parallel",)),
        name="router",
    )(x, w_router_t, bias_col)


def _experts_kernel(pe_ref, pb_ref, lo_ref, hi_ref, tok_ref, dst_ref, nxt_ref, x_hbm, wt_ref, wg_ref, wu_ref, wd_ref,
                    y_hbm, xin, yout, sem_in, sem_out, *, tb, n_assign):
    pair = pl.program_id(0)
    n_used = pe_ref[pe_ref.shape[0] - 1]
    slot = pair % 2

    def gather(ids_ref, buf):
        def body(i, _):
            pltpu.make_async_copy(x_hbm.at[pl.ds(ids_ref[0, i], 1)], xin.at[buf, pl.ds(i, 1)], sem_in.at[buf]).start()
            return 0

        lax.fori_loop(0, tb, body, 0, unroll=8)

    def wait_scatter():
        pltpu.make_async_copy(yout, y_hbm.at[pl.ds(0, tb)], sem_out).wait()

    @pl.when(pair < n_used)
    def _():
        @pl.when(pair == 0)
        def _():
            gather(tok_ref, 0)

        pltpu.make_async_copy(x_hbm.at[pl.ds(0, tb)], xin.at[slot], sem_in.at[slot]).wait()

        @pl.when(pair + 1 < n_used)
        def _():
            gather(nxt_ref, 1 - slot)

        xb = xin[slot].astype(BF16)
        hg = jnp.dot(xb, wg_ref[...].astype(BF16), preferred_element_type=F32)
        hu = jnp.dot(xb, wu_ref[...].astype(BF16), preferred_element_type=F32)
        hidden = hg * _sigmoid(hg) * hu
        y = jnp.dot(hidden.astype(BF16), wd_ref[...].astype(BF16), preferred_element_type=F32) * wt_ref[...]

        @pl.when(pair > 0)
        def _():
            wait_scatter()

        yout[...] = y
        lo, hi = lo_ref[pair], hi_ref[pair]

        def scatter(i, _):
            row = jnp.where((i >= lo) & (i < hi), dst_ref[0, i], n_assign + i)
            pltpu.make_async_copy(yout.at[pl.ds(i, 1)], y_hbm.at[pl.ds(row, 1)], sem_out).start()
            return 0

        lax.fori_loop(0, tb, scatter, 0, unroll=8)

        @pl.when(pair == n_used - 1)
        def _():
            wait_scatter()


def _experts(x, pairs, tok_sorted, a_sorted, w_sorted, w_e_gate, w_e_up, w_e_down, *, tb):
    pair_expert, pair_block, lo, hi = pairs
    n_pairs = pair_block.shape[0]
    n_assign = a_sorted.shape[0] * tb
    wspec = lambda a: pl.BlockSpec((None,) + a.shape[1:], lambda i, pe, pb, lo, hi: (pe[i], 0, 0))
    ids = pl.BlockSpec((None, 1, tb), lambda i, pe, pb, lo, hi: (pb[i], 0, 0), memory_space=pltpu.SMEM)
    ids_next = pl.BlockSpec((None, 1, tb), lambda i, pe, pb, lo, hi: (pb[jnp.minimum(i + 1, n_pairs - 1)], 0, 0),
                            memory_space=pltpu.SMEM)
    grid_spec = pltpu.PrefetchScalarGridSpec(
        num_scalar_prefetch=4,
        grid=(n_pairs,),
        in_specs=[ids, ids, ids_next, pl.BlockSpec(memory_space=pl.ANY),
                  pl.BlockSpec((tb, 1), lambda i, pe, pb, lo, hi: (pb[i], 0)),
                  wspec(w_e_gate), wspec(w_e_up), wspec(w_e_down)],
        out_specs=pl.BlockSpec(memory_space=pl.ANY),
        scratch_shapes=[pltpu.VMEM((2, tb, D_MODEL), F32), pltpu.VMEM((tb, D_MODEL), F32),
                        pltpu.SemaphoreType.DMA((2,)), pltpu.SemaphoreType.DMA(())],
    )
    return pl.pallas_call(
        functools.partial(_experts_kernel, tb=tb, n_assign=n_assign),
        out_shape=jax.ShapeDtypeStruct((n_assign + tb, D_MODEL), F32),
        grid_spec=grid_spec,
        compiler_params=_cparams(("arbitrary",)),
        name="experts",
    )(pair_expert, pair_block, lo, hi, tok_sorted, a_sorted, tok_sorted, x, w_sorted, w_e_gate, w_e_up, w_e_down)


def _dispatch(top_idx, top_w, n_tok, tb):
    n_assign = TOP_K * n_tok
    n_blocks = n_assign // tb
    n_pairs = n_blocks + N_EXPERTS
    a_idx = jnp.arange(n_assign, dtype=jnp.int32)
    e_sorted, a_sorted, w_sorted = lax.sort((top_idx.reshape(-1), a_idx, top_w.reshape(-1)), num_keys=1)
    experts = jnp.arange(N_EXPERTS, dtype=jnp.int32)
    start = jnp.searchsorted(e_sorted, experts, side="left").astype(jnp.int32)
    end = jnp.searchsorted(e_sorted, experts, side="right").astype(jnp.int32)
    visits = jnp.where(end > start, (end - 1) // tb - start // tb + 1, 0)
    visit_end = jnp.cumsum(visits)
    pair = jnp.arange(n_pairs, dtype=jnp.int32)
    pair_expert = jnp.minimum(jnp.searchsorted(visit_end, pair, side="right"), N_EXPERTS - 1).astype(jnp.int32)
    pair_block = start[pair_expert] // tb + pair - (visit_end - visits)[pair_expert]
    pair_block = jnp.clip(pair_block, 0, n_blocks - 1).astype(jnp.int32)
    lo = jnp.clip(start[pair_expert] - pair_block * tb, 0, tb).astype(jnp.int32)
    hi = jnp.clip(end[pair_expert] - pair_block * tb, 0, tb).astype(jnp.int32)
    n_used = visit_end[-1].astype(jnp.int32)
    pairs = (jnp.concatenate([pair_expert, n_used[None]]), pair_block, lo, hi)
    blocks = lambda v: v.reshape(n_blocks, 1, tb)
    return pairs, blocks(a_sorted % n_tok), blocks(a_sorted), w_sorted.reshape(n_assign, 1)


def _ffn_out_kernel(x_ref, wg_ref, wu_ref, wd_ref, g_ref, b_ref, *rest):
    yr_refs, o_ref = rest[:-1], rest[-1]
    x = x_ref[...]
    xb = x.astype(BF16)
    hg = jnp.dot(xb, wg_ref[...], preferred_element_type=F32)
    hu = jnp.dot(xb, wu_ref[...], preferred_element_type=F32)
    ffn = jnp.dot((hg * _sigmoid(hg) * hu).astype(BF16), wd_ref[...], preferred_element_type=F32)
    for yr_ref in yr_refs:
        ffn = ffn + yr_ref[...]
    o_ref[...] = _layer_norm(ALPHA * x + ffn, g_ref[...], b_ref[...])


def _ffn_out(x, y_routed, w_s_gate, w_s_up, w_s_down, ln_g, ln_b, tm):
    n = x.shape[0]
    full = lambda a: pl.BlockSpec(a.shape, lambda i: (0,) * a.ndim)
    row = pl.BlockSpec((tm, D_MODEL), lambda i: (i, 0))
    slab = lambda j: pl.BlockSpec((tm, D_MODEL), lambda i, j=j: (j * (n // tm) + i, 0))
    return pl.pallas_call(
        _ffn_out_kernel,
        out_shape=jax.ShapeDtypeStruct((n, D_MODEL), F32),
        grid=(n // tm,),
        in_specs=[row, full(w_s_gate), full(w_s_up), full(w_s_down), full(ln_g), full(ln_b)]
        + [slab(j) for j in range(TOP_K)],
        out_specs=row,
        compiler_params=_cparams(("parallel",)),
        name="ffn_out",
    )(x, w_s_gate, w_s_up, w_s_down, ln_g, ln_b, *([y_routed] * TOP_K))


WKV_CHUNK = 64
SB_TILE = 384
SB_HEAD_PAIRS = 2
SEQ_PAD = 768
ROW_TILE = 512
EXPERT_ROWS = 128
ROUTER_TILE = 256
SAMPLE_PAGES_PER_STEP = 8


def _pages_keys_minor(cache):
    n_pool = cache.shape[0]
    return jnp.transpose(cache, (0, 2, 3, 1)).reshape(n_pool, WIDTH, PAGE_SIZE)


def _shift_rows(p_rwkv, first):
    return jnp.concatenate([first[:, None, :], p_rwkv[:, :-1]], axis=1)


def kernel(x_prompt, x_sample, cache_k, cache_v, page_table, state_wkv, state_shift, meta_tokens, w_in, mu_shift, w0,
           w_decay_up, a0, w_a_up, w_g_up, k_k, k_a, r_k, gn_g, gn_b, sb_bias, w_proj_a, w_proj_b, w_o, ln1_g, ln1_b,
           w_router, router_bias, w_e_gate, w_e_up, w_e_down, w_s_gate, w_s_up, w_s_down, ln2_g, ln2_b):
    assert w_in.shape[0] == DEPTH == 1
    n_p, seq = x_prompt.shape[:2]
    n_s, t_s = x_sample.shape[:2]
    lp = N_META + seq
    lp_pad = -(-lp // SEQ_PAD) * SEQ_PAD
    rows_p = n_p * lp_pad
    rows_s = n_s * t_s
    n_tok = rows_p + rows_s
    assert lp_pad % WKV_CHUNK == 0 and lp_pad % SB_TILE == 0 and n_tok % ROW_TILE == 0 and rows_p % t_s == 0

    meta = jnp.broadcast_to(meta_tokens[None].astype(F32), (n_p, N_META, D_MODEL))
    xp = jnp.concatenate([meta, x_prompt, jnp.zeros((n_p, lp_pad - lp, D_MODEL), F32)], axis=1)
    x = jnp.concatenate([xp.reshape(rows_p, D_MODEL), x_sample.reshape(rows_s, D_MODEL)], axis=0)
    valid = jnp.concatenate([jnp.broadcast_to((jnp.arange(lp_pad) < lp)[None], (n_p, lp_pad)).reshape(rows_p),
                             jnp.ones((rows_s,), jnp.bool_)]).astype(F32)[:, None]

    w = w_in[0]
    w_cols = jnp.concatenate([w[:, :3 * WIDTH], w[:, P_RWKV:], w[:, 3 * WIDTH:P_RWKV]], axis=1)
    p = _project(x, w_cols.astype(BF16), ROW_TILE, 768)
    p_rwkv = jnp.concatenate([p[:, :3 * WIDTH], p[:, COL_LORA:]], axis=1)
    pp3 = p_rwkv[:rows_p].reshape(n_p, lp_pad, P_RWKV)
    ps3 = p_rwkv[rows_p:].reshape(n_s, t_s, P_RWKV)

    p_prev = jnp.concatenate([
        _shift_rows(pp3, jnp.zeros((n_p, P_RWKV), F32)).reshape(rows_p, P_RWKV),
        _shift_rows(ps3, state_shift[0]).reshape(rows_s, P_RWKV)], axis=0)
    zpad = lambda w, lo: jnp.zeros((LORA, WIDTH), F32).at[lo:lo + w.shape[0]].set(w)
    vec = lambda a: a.reshape(1, -1)
    r, lw, k, v, a, g = _rwkv_pre(p, p_prev, valid, mu_shift, vec(w0[0]), zpad(w_decay_up[0], 0), vec(a0[0]),
                                  zpad(w_a_up[0], DECAY_LORA), zpad(w_g_up[0], DECAY_LORA + A_LORA), ROW_TILE)
    head_vecs = (vec(k_k[0]), vec(k_a[0]), vec(r_k[0]), vec(gn_g[0]), vec(gn_b[0]))
    ya_p, wkv_p = _wkv_scan(r, lw, k, v, a, g, jnp.zeros((n_p,) + state_wkv.shape[2:], F32), *head_vecs,
                            row0=0, seq_stride=lp_pad, n_seq=n_p, n_chunks=lp_pad // WKV_CHUNK, chunk=WKV_CHUNK,
                            heads=8, passes=3)
    ya_s, wkv_s = _wkv_scan(r, lw, k, v, a, g, state_wkv[0], *head_vecs,
                            row0=rows_p, seq_stride=t_s, n_seq=n_s, n_chunks=1, chunk=t_s, heads=N_HEADS, passes=3)
    y_a = jnp.concatenate([ya_p, ya_s], axis=0)

    qkv_s = p[rows_p:, COL_SB:COL_GATE].reshape(rows_s, 3, WIDTH)
    yb_p = _sb_prompt(p, p[:, COL_SB + WIDTH:COL_GATE].astype(BF16), sb_bias[0], n_seq=n_p, seq_stride=lp_pad,
                      tile=SB_TILE, pairs=SB_HEAD_PAIRS)
    yb_s = _sb_sample(qkv_s[:, 0], qkv_s[:, 1], qkv_s[:, 2], jnp.repeat(sb_bias[0], t_s)[:, None],
                      _pages_keys_minor(cache_k[0]), _pages_keys_minor(cache_v[0]),
                      page_table, n_pg=SAMPLE_PAGES_PER_STEP)
    y_b = jnp.concatenate([yb_p, yb_s], axis=0)

    x1 = _merge(x, y_a, y_b, p, w_proj_a[0].astype(BF16), w_proj_b[0].astype(BF16), w_o[0].astype(BF16),
                vec(ln1_g[0]), vec(ln1_b[0]), ROW_TILE)

    top_idx, top_w = _router(x1, w_router[0].T, router_bias[0][:, None], ROUTER_TILE)
    pairs, tok_sorted, a_sorted, w_sorted = _dispatch(top_idx, top_w, n_tok, EXPERT_ROWS)
    y_routed = _experts(x1, pairs, tok_sorted, a_sorted, w_sorted, w_e_gate[0], w_e_up[0], w_e_down[0], tb=EXPERT_ROWS)
    x2 = _ffn_out(x1, y_routed, w_s_gate[0].astype(BF16), w_s_up[0].astype(BF16), w_s_down[0].astype(BF16),
                  vec(ln2_g[0]), vec(ln2_b[0]), ROW_TILE)

    x2p = x2[:rows_p].reshape(n_p, lp_pad, D_MODEL)
    kv_p = p[:rows_p, COL_SB + WIDTH:COL_GATE].reshape(n_p, lp_pad, 2, N_HEADS, HEAD_DIM)[:, :lp]
    qkv_s5 = qkv_s.reshape(n_s, t_s, 3, N_HEADS, HEAD_DIM)
    return (x2p[:, N_META:lp], x2[rows_p:].reshape(n_s, t_s, D_MODEL),
            kv_p[None, :, :, 0], kv_p[None, :, :, 1], qkv_s5[None, :, :, 1], qkv_s5[None, :, :, 2],
            wkv_p[None], pp3[:, lp - 1][None], wkv_s[None], ps3[:, t_s - 1][None])
```

```python
import functools

import jax
import jax.numpy as jnp
from jax import lax
from jax.experimental import pallas as pl
from jax.experimental.pallas import tpu as pltpu

F32 = jnp.float32
BF16 = jnp.bfloat16

D_MODEL = 1024
N_META = 16
HEAD_DIM = 64
N_HEADS = 16
WIDTH = N_HEADS * HEAD_DIM
DECAY_LORA = 64
A_LORA = 64
GATE_LORA = 128
LORA = DECAY_LORA + A_LORA + GATE_LORA
P_RWKV = 3 * WIDTH + LORA
P_SB = 3 * WIDTH
P_GATE = 2 * D_MODEL
P_IN = P_RWKV + P_SB + P_GATE
PAGE_SIZE = 128
N_EXPERTS = 256
TOP_K = 8
EXPERT_HIDDEN = 256
N_EXPERT_GROUPS = 8
GROUP_SIZE = N_EXPERTS // N_EXPERT_GROUPS
TOPK_GROUPS = 4
ROUTED_SCALE = 2.5
DEPTH = 1
ALPHA = (2.0 * DEPTH) ** 0.25
LN_EPS = 1e-5
GN_EPS = 64e-5

COL_RWKV = 0
COL_SB = 3 * WIDTH
COL_GATE = COL_SB + P_SB
COL_LORA = COL_GATE + P_GATE

LANE = 128
VMEM_LIMIT = 56 * 1024 * 1024


def _cparams(sem):
    return pltpu.CompilerParams(dimension_semantics=sem, vmem_limit_bytes=VMEM_LIMIT)


def _split(x):
    hi = x.astype(BF16)
    lo = (x - hi.astype(F32)).astype(BF16)
    return hi, lo


_NN = (((1,), (0,)), ((), ()))
_NT = (((1,), (1,)), ((), ()))
_TN = (((0,), (0,)), ((), ()))


def _dot(a, b, dims=_NN, passes=1):
    dg = lambda x, y: lax.dot_general(x, y, dims, preferred_element_type=F32)
    if passes == 1:
        return dg(a.astype(BF16), b.astype(BF16))
    ah, al = _split(a)
    bh, bl = _split(b)
    return dg(ah, bh) + (dg(ah, bl) + dg(al, bh))


def _dot_exact_lhs(a_bf16, b, dims=_NN):
    bh, bl = _split(b)
    dg = lambda x, y: lax.dot_general(x, y, dims, preferred_element_type=F32)
    return dg(a_bf16, bh) + dg(a_bf16, bl)


def _softplus(x):
    return jnp.maximum(x, 0.0) + jnp.log1p(jnp.exp(-jnp.abs(x)))


def _sigmoid(x):
    return 1.0 / (1.0 + jnp.exp(-x))


def _layer_norm(x, g, b):
    mu = jnp.mean(x, axis=-1, keepdims=True)
    xc = x - mu
    var = jnp.mean(xc * xc, axis=-1, keepdims=True)
    return xc * lax.rsqrt(var + LN_EPS) * g + b


def _proj_kernel(x_ref, w_ref, o_ref):
    o_ref[...] = jnp.dot(x_ref[...].astype(BF16), w_ref[...], preferred_element_type=F32)


def _project(x, w_bf16, tm, tn):
    n, k = x.shape
    m = w_bf16.shape[1]
    return pl.pallas_call(
        _proj_kernel,
        out_shape=jax.ShapeDtypeStruct((n, m), F32),
        grid=(n // tm, m // tn),
        in_specs=[pl.BlockSpec((tm, k), lambda i, j: (i, 0)), pl.BlockSpec((k, tn), lambda i, j: (0, j))],
        out_specs=pl.BlockSpec((tm, tn), lambda i, j: (i, j)),
        compiler_params=_cparams(("parallel", "parallel")),
        name="in_proj",
    )(x, w_bf16)


def _rwkv_pre_kernel(pr, pk, pv, pl_, qr, qk, qv, ql, valid, mur, muk, muv, mul, w0, wd, a0, wa, wg,
                     r_o, lw_o, k_o, v_o, a_o, g_o):
    ok = valid[...]
    mix = lambda p, q, mu: (p[...] + (q[...] - p[...]) * mu[...])
    r_o[...] = mix(pr, qr, mur) * ok
    k_o[...] = mix(pk, qk, muk) * ok
    v_o[...] = mix(pv, qv, muv) * ok
    lo = mix(pl_, ql, mul)
    w_log = -_softplus(-(w0[...] + _dot(jnp.tanh(lo), wd[...], passes=3))) - 0.5
    lw_o[...] = -jnp.exp(w_log) * ok
    a_o[...] = _sigmoid(a0[...] + _dot(lo, wa[...], passes=3))
    g_o[...] = _dot(_sigmoid(lo), wg[...], passes=3)


def _rwkv_pre(p, p_prev, valid, mu_shift, w0, wd_pad, a0, wa_pad, wg_pad, tm):
    n = p.shape[0]
    row = lambda w, c: pl.BlockSpec((tm, w), lambda i, c=c: (i, c))
    vec = lambda w, c: pl.BlockSpec((1, w), lambda i, c=c: (0, c))
    full = lambda a: pl.BlockSpec(a.shape, lambda i: (0,) * a.ndim)
    rkv = [row(WIDTH, 0), row(WIDTH, 1), row(WIDTH, 2)]
    prev_lora = 3 * WIDTH // LORA
    in_specs = rkv + [row(LORA, COL_LORA // LORA)] + rkv + [row(LORA, prev_lora)] + [
        pl.BlockSpec((tm, 1), lambda i: (i, 0)),
        vec(WIDTH, 0), vec(WIDTH, 1), vec(WIDTH, 2), vec(LORA, prev_lora),
        full(w0), full(wd_pad), full(a0), full(wa_pad), full(wg_pad)]
    out = jax.ShapeDtypeStruct((n, WIDTH), F32)
    return pl.pallas_call(
        _rwkv_pre_kernel,
        out_shape=(out,) * 6,
        grid=(n // tm,),
        in_specs=in_specs,
        out_specs=(pl.BlockSpec((tm, WIDTH), lambda i: (i, 0)),) * 6,
        compiler_params=_cparams(("parallel",)),
        name="rwkv_pre",
    )(p, p, p, p, p_prev, p_prev, p_prev, p_prev, valid, mu_shift, mu_shift, mu_shift, mu_shift,
      w0, wd_pad, a0, wa_pad, wg_pad)


def _wkv_kernel(r_ref, lw_ref, k_ref, v_ref, a_ref, g_ref, s0_ref, kk_ref, ka_ref, rk_ref, gg_ref, gb_ref,
                y_ref, s_ref, *, chunk, heads, passes):
    c = pl.program_id(2)

    @pl.when(c == 0)
    def _():
        s_ref[...] = s0_ref[...]

    ti = lax.broadcasted_iota(jnp.int32, (chunk, chunk), 0)
    tj = lax.broadcasted_iota(jnp.int32, (chunk, chunk), 1)
    incl = ti >= tj
    strict = ti > tj
    tri_incl = jnp.where(incl, 1.0, 0.0).astype(BF16)
    eye = jnp.where(ti == tj, 1.0, 0.0)
    mm = functools.partial(_dot, passes=passes)

    hs = range(heads)
    cols = [slice(h * HEAD_DIM, (h + 1) * HEAD_DIM) for h in hs]
    load = lambda ref: [ref[:, c] for c in cols]
    r, lw, k, v, a, g = (load(ref) for ref in (r_ref, lw_ref, k_ref, v_ref, a_ref, g_ref))
    kk_w, ka_w, rk_w, gg_w, gb_w = (load(ref) for ref in (kk_ref, ka_ref, rk_ref, gg_ref, gb_ref))
    kk = [k[h] * kk_w[h] for h in hs]
    kk = [kk[h] / jnp.maximum(jnp.sqrt(jnp.sum(kk[h] * kk[h], axis=-1, keepdims=True)), 1e-12) for h in hs]
    k2 = [k[h] * (1.0 + (a[h] - 1.0) * ka_w[h]) for h in hs]
    b_vec = [kk[h] * a[h] for h in hs]
    cl = [_dot_exact_lhs(tri_incl, lw[h]) for h in hs]
    cl_end = [cl[h][chunk - 1:chunk, :] for h in hs]
    p_inv = [jnp.exp(-cl[h]) for h in hs]
    p_end = [jnp.exp(cl_end[h] - cl[h]) for h in hs]
    at = [-kk[h] * jnp.exp(cl[h] - lw[h]) for h in hs]
    rt = [r[h] * jnp.exp(cl[h]) for h in hs]
    bt = [b_vec[h] * p_inv[h] for h in hs]
    kt = [k2[h] * p_inv[h] for h in hs]
    l_ab = [jnp.where(strict, mm(at[h], bt[h], _NT), 0.0) for h in hs]
    l_ak = [jnp.where(strict, mm(at[h], kt[h], _NT), 0.0) for h in hs]
    m_rb = [jnp.where(incl, _dot(rt[h], bt[h], _NT), 0.0) for h in hs]
    m_rk = [jnp.where(incl, _dot(rt[h], kt[h], _NT), 0.0) for h in hs]
    inv = [eye + l_ab[h] for h in hs]
    pw = l_ab
    n = 2
    while n < chunk:
        pw = [mm(pw[h], pw[h]) for h in hs]
        inv = [inv[h] + mm(inv[h], pw[h]) for h in hs]
        n *= 2
    s0 = [s_ref[0, h] for h in hs]
    x = [mm(at[h], s0[h], _NT) + mm(l_ak[h], v[h]) for h in hs]
    u = [mm(inv[h], x[h]) for h in hs]
    y = [_dot(rt[h], s0[h], _NT) + _dot(m_rb[h], u[h]) + _dot(m_rk[h], v[h]) for h in hs]
    s_new = [s0[h] * jnp.exp(cl_end[h]) + mm(u[h], b_vec[h] * p_end[h], _TN) + mm(v[h], k2[h] * p_end[h], _TN)
             for h in hs]
    for h in hs:
        s_ref[0, h] = s_new[h]
        mu = jnp.mean(y[h], axis=-1, keepdims=True)
        yc = y[h] - mu
        var = jnp.mean(yc * yc, axis=-1, keepdims=True)
        yn = yc * lax.rsqrt(var + GN_EPS) * gg_w[h] + gb_w[h]
        bonus = jnp.sum(r[h] * k2[h] * rk_w[h], axis=-1, keepdims=True) * v[h]
        y_ref[:, cols[h]] = (yn + bonus) * g[h]


def _wkv_scan(r, lw, k, v, a, g, s0, k_k, k_a, r_k, gn_g, gn_b, *, row0, seq_stride, n_seq, n_chunks, chunk,
              heads, passes):
    n_rows = n_seq * seq_stride
    hb = N_HEADS // heads
    wblk = heads * HEAD_DIM
    blk0, blk_stride = row0 // chunk, seq_stride // chunk
    row = pl.BlockSpec((chunk, wblk), lambda s, j, c: (blk0 + s * blk_stride + c, j))
    vec = pl.BlockSpec((1, wblk), lambda s, j, c: (0, j))
    st = pl.BlockSpec((1, heads, HEAD_DIM, HEAD_DIM), lambda s, j, c: (s, j, 0, 0))
    y, s_new = pl.pallas_call(
        functools.partial(_wkv_kernel, chunk=chunk, heads=heads, passes=passes),
        out_shape=(jax.ShapeDtypeStruct((n_rows, WIDTH), F32), jax.ShapeDtypeStruct(s0.shape, F32)),
        grid=(n_seq, hb, n_chunks),
        in_specs=[row] * 6 + [st] + [vec] * 5,
        out_specs=(pl.BlockSpec((chunk, wblk), lambda s, j, c: (s * blk_stride + c, j)), st),
        compiler_params=_cparams(("parallel", "parallel", "arbitrary")),
        name="wkv_scan",
    )(r, lw, k, v, a, g, s0, k_k, k_a, r_k, gn_g, gn_b)
    return y, s_new


def _sb_scores(z, vis, tri2):
    sp = jnp.maximum(z, 0.0) + jnp.log(1.0 + jnp.exp(-jnp.abs(z)))
    ls = z - sp
    if vis is not None:
        sp = jnp.where(vis, sp, 0.0)
    later = jnp.dot(jnp.concatenate(_split(sp), axis=1), tri2, preferred_element_type=F32)
    return ls, later, jnp.sum(sp, axis=-1, keepdims=True)


def _sb_weights(ls, later, dead, vis):
    w = jnp.exp(ls - (dead + later))
    return w if vis is None else jnp.where(vis, w, 0.0)


DEAD_MAX = 150.0


def _later_key_matrix():
    m = lax.broadcasted_iota(jnp.int32, (LANE, LANE), 0)
    j = lax.broadcasted_iota(jnp.int32, (LANE, LANE), 1)
    tri = jnp.where(m > j, 1.0, 0.0).astype(BF16)
    return jnp.concatenate([tri, tri], axis=0)


def _sb_prompt_kernel(bias_ref, q_ref, k_ref, v_ref, o_ref, *, tile, pairs):
    hp, qi = pl.program_id(1), pl.program_id(2)
    groups = tile // LANE
    tri2 = _later_key_matrix()
    lane = lax.broadcasted_iota(jnp.int32, (1, LANE), 1)
    row = lax.broadcasted_iota(jnp.int32, (tile, LANE), 0)
    col = lax.broadcasted_iota(jnp.int32, (tile, LANE), 1)
    n_h = 2 * pairs
    qh, bias = [], []
    for h in range(n_h):
        q = q_ref[:, (h // 2) * LANE:(h // 2 + 1) * LANE] * (HEAD_DIM ** -0.5)
        own = (lane >= (h % 2) * HEAD_DIM) & (lane < (h % 2 + 1) * HEAD_DIM)
        qh.append(jnp.where(own, q, 0.0).astype(BF16))
        bias.append(bias_ref[n_h * hp + h])
    units = [(h, g) for h in range(n_h) for g in range(groups)]

    def block(j, masked, carry):
        base_row = pl.multiple_of(j * tile, tile)
        kb = {(c, g): k_ref[pl.ds(base_row + g * LANE, LANE), c * LANE:(c + 1) * LANE]
              for c in range(pairs) for g in range(groups)}
        vblk = [v_ref[pl.ds(base_row, tile), c * LANE:(c + 1) * LANE] for c in range(pairs)]
        vis = [row > col + g * LANE for g in range(groups)] if masked else None
        z = {u: lax.dot_general(qh[u[0]], kb[u[0] // 2, u[1]], _NT, preferred_element_type=F32) + bias[u[0]]
             for u in units}
        sc = {u: _sb_scores(z[u], vis[u[1]] if masked else None, tri2) for u in units}
        out = []
        for h in range(n_h):
            dead, acc = carry[h]
            w = [None] * groups
            for g in reversed(range(groups)):
                ls, later, total = sc[h, g]
                w[g] = _sb_weights(ls, later, dead, vis[g] if masked else None).astype(BF16)
                dead = dead + total
            acc = acc + jnp.dot(jnp.concatenate(w, axis=1), vblk[h // 2], preferred_element_type=F32)
            out.append((dead, acc))
        return tuple(out)

    def alive(carry):
        least = carry[0][0]
        for dead, _ in carry[1:]:
            least = jnp.minimum(least, dead)
        return jnp.min(least) < DEAD_MAX

    zero = (jnp.zeros((tile, 1), F32), jnp.zeros((tile, LANE), F32))
    carry = block(qi, True, (zero,) * n_h)
    _, _, carry = lax.while_loop(
        lambda s: (s[0] < qi) & s[1],
        lambda s: (lambda c: (s[0] + 1, alive(c), c))(block(qi - 1 - s[0], False, s[2])),
        (0, alive(carry), carry))
    for c in range(pairs):
        o_ref[:, c * LANE:(c + 1) * LANE] = jnp.where(lane < HEAD_DIM, carry[2 * c][1], carry[2 * c + 1][1])


def _sb_prompt(p, kv_bf16, sb_bias, *, n_seq, seq_stride, tile, pairs):
    n_q = seq_stride // tile
    wide = pairs * LANE
    qc = COL_SB // wide
    grid_spec = pltpu.PrefetchScalarGridSpec(
        num_scalar_prefetch=1,
        grid=(n_seq, WIDTH // wide, n_q),
        in_specs=[pl.BlockSpec((tile, wide), lambda b, j, i, bias: (b * n_q + i, qc + j)),
                  pl.BlockSpec((seq_stride, wide), lambda b, j, i, bias: (b, j)),
                  pl.BlockSpec((seq_stride, wide), lambda b, j, i, bias: (b, WIDTH // wide + j))],
        out_specs=pl.BlockSpec((tile, wide), lambda b, j, i, bias: (b * n_q + i, j)),
    )
    return pl.pallas_call(
        functools.partial(_sb_prompt_kernel, tile=tile, pairs=pairs),
        out_shape=jax.ShapeDtypeStruct((n_seq * seq_stride, WIDTH), F32),
        grid_spec=grid_spec,
        compiler_params=_cparams(("parallel", "parallel", "arbitrary")),
        name="sb_prompt",
    )(sb_bias, p, kv_bf16, kv_bf16)


def _sb_sample_kernel(pt_ref, q_ref, kn_ref, vn_ref, bias_ref, *rest, n_pg, t_new):
    k_pages, v_pages = rest[:n_pg], rest[n_pg:2 * n_pg]
    o_ref, acc_ref, suf_ref = rest[2 * n_pg:]
    s = pl.program_id(1)
    rows = N_HEADS * t_new
    tri2 = _later_key_matrix()
    rh = lax.broadcasted_iota(jnp.int32, (rows, WIDTH), 0) // t_new
    ch = lax.broadcasted_iota(jnp.int32, (rows, WIDTH), 1) // HEAD_DIM
    own = rh == ch
    q = q_ref[...] * (HEAD_DIM ** -0.5)
    q_bd = jnp.where(own, jnp.concatenate([q] * N_HEADS, axis=0), 0.0).astype(BF16)
    bias = bias_ref[...]

    @pl.when(s == 0)
    def _():
        pad = jnp.zeros((PAGE_SIZE - t_new, WIDTH), F32)
        kn = jnp.concatenate([kn_ref[...], pad], axis=0).astype(BF16)
        vn = jnp.concatenate([vn_ref[...], pad], axis=0).astype(BF16)
        key = lax.broadcasted_iota(jnp.int32, (rows, PAGE_SIZE), 1)
        vis = key < lax.broadcasted_iota(jnp.int32, (rows, PAGE_SIZE), 0) % t_new
        z = lax.dot_general(q_bd, kn, _NT, preferred_element_type=F32) + bias
        ls, later, total = _sb_scores(z, vis, tri2)
        w = _sb_weights(ls, later, 0.0, vis)
        acc_ref[...] = jnp.dot(w.astype(BF16), vn, preferred_element_type=F32)
        suf_ref[...] = total

    sc = [_sb_scores(jnp.dot(q_bd, k_pages[i][...].astype(BF16), preferred_element_type=F32) + bias, None, tri2)
          for i in range(n_pg)]
    dead = suf_ref[...]
    acc = acc_ref[...]
    for i in range(n_pg):
        ls, later, total = sc[i]
        w = _sb_weights(ls, later, dead, None)
        acc = acc + lax.dot_general(w.astype(BF16), v_pages[i][...].astype(BF16), _NT, preferred_element_type=F32)
        dead = dead + total
    suf_ref[...] = dead
    acc_ref[...] = acc

    @pl.when(s == pl.num_programs(1) - 1)
    def _():
        acc = jnp.where(own, acc_ref[...], 0.0)
        out = acc[0:t_new]
        for h in range(1, N_HEADS):
            out = out + acc[h * t_new:(h + 1) * t_new]
        o_ref[...] = out


def _sb_sample(q, k_new, v_new, bias_rows, cache_k, cache_v, page_table, *, n_pg):
    n_seq, n_pages = page_table.shape
    t_new = q.shape[0] // n_seq
    rows = N_HEADS * t_new
    tok = pl.BlockSpec((t_new, WIDTH), lambda b, s, pt: (b, 0))

    def page_spec(i):
        return pl.BlockSpec((None, WIDTH, PAGE_SIZE), lambda b, s, pt, i=i: (pt[b, n_pages - 1 - (s * n_pg + i)], 0, 0))

    grid_spec = pltpu.PrefetchScalarGridSpec(
        num_scalar_prefetch=1,
        grid=(n_seq, n_pages // n_pg),
        in_specs=[tok, tok, tok, pl.BlockSpec((rows, 1), lambda b, s, pt: (0, 0))]
        + [page_spec(i) for i in range(n_pg)] * 2,
        out_specs=tok,
        scratch_shapes=[pltpu.VMEM((rows, WIDTH), F32), pltpu.VMEM((rows, 1), F32)],
    )
    return pl.pallas_call(
        functools.partial(_sb_sample_kernel, n_pg=n_pg, t_new=t_new),
        out_shape=jax.ShapeDtypeStruct(q.shape, F32),
        grid_spec=grid_spec,
        compiler_params=_cparams(("parallel", "arbitrary")),
        name="sb_sample",
    )(page_table, q, k_new, v_new, bias_rows, *([cache_k] * n_pg), *([cache_v] * n_pg))


def _merge_kernel(x_ref, ya_ref, yb_ref, ga_ref, gb_ref, wa_ref, wb_ref, wo_ref, g_ref, b_ref, o_ref):
    ma = _sigmoid(ga_ref[...]) * _dot(ya_ref[...], wa_ref[...])
    mb = _sigmoid(gb_ref[...]) * _dot(yb_ref[...], wb_ref[...])
    mixed = _dot(ma + mb, wo_ref[...])
    o_ref[...] = _layer_norm(ALPHA * x_ref[...] + mixed, g_ref[...], b_ref[...])


def _merge(x, y_a, y_b, p, w_proj_a, w_proj_b, w_o, ln_g, ln_b, tm):
    n = x.shape[0]
    row = lambda c: pl.BlockSpec((tm, D_MODEL), lambda i, c=c: (i, c))
    full = lambda a: pl.BlockSpec(a.shape, lambda i: (0,) * a.ndim)
    gate_blk = COL_GATE // D_MODEL
    return pl.pallas_call(
        _merge_kernel,
        out_shape=jax.ShapeDtypeStruct((n, D_MODEL), F32),
        grid=(n // tm,),
        in_specs=[row(0), row(0), row(0), row(gate_blk), row(gate_blk + 1),
                  full(w_proj_a), full(w_proj_b), full(w_o), full(ln_g), full(ln_b)],
        out_specs=row(0),
        compiler_params=_cparams(("parallel",)),
        name="merge_ln1",
    )(x, y_a, y_b, p, p, w_proj_a, w_proj_b, w_o, ln_g, ln_b)


def _router_kernel(x_ref, wr_ref, bias_ref, idx_ref, wgt_ref):
    tm = x_ref.shape[0]
    logits = _dot(wr_ref[...], x_ref[...], _NT, passes=3)
    s = _sigmoid(logits)
    biased = s + bias_ref[...]
    neg = -jnp.inf
    eid = lax.broadcasted_iota(jnp.int32, (N_EXPERTS, tm), 0)
    gid = eid // GROUP_SIZE

    def take_max(cand, ids, n_ids):
        m = jnp.max(cand, axis=0, keepdims=True)
        first = jnp.min(jnp.where(cand == m, ids, n_ids), axis=0, keepdims=True)
        return m, first

    gids = lax.broadcasted_iota(jnp.int32, (N_EXPERT_GROUPS, tm), 0)
    gs = jnp.full((N_EXPERT_GROUPS, tm), neg, F32)
    for gidx in range(N_EXPERT_GROUPS):
        blk = biased[gidx * GROUP_SIZE:(gidx + 1) * GROUP_SIZE]
        ids = lax.broadcasted_iota(jnp.int32, (GROUP_SIZE, tm), 0) + gidx * GROUP_SIZE
        m1, i1 = take_max(blk, ids, N_EXPERTS)
        m2, _ = take_max(jnp.where(ids == i1, neg, blk), ids, N_EXPERTS)
        gs = jnp.where(gids == gidx, m1 + m2, gs)
    cand = jnp.full((N_EXPERTS, tm), neg, F32)
    for _ in range(TOPK_GROUPS):
        _, gi = take_max(gs, gids, N_EXPERT_GROUPS)
        gs = jnp.where(gids == gi, neg, gs)
        cand = jnp.where(gid == gi, biased, cand)
    kids = lax.broadcasted_iota(jnp.int32, (TOP_K, tm), 0)
    idx = jnp.zeros((TOP_K, tm), jnp.int32)
    sel = jnp.zeros((TOP_K, tm), F32)
    for j in range(TOP_K):
        _, ei = take_max(cand, eid, N_EXPERTS)
        hit = eid == ei
        idx = jnp.where(kids == j, ei, idx)
        sel = jnp.where(kids == j, jnp.sum(jnp.where(hit, s, 0.0), axis=0, keepdims=True), sel)
        cand = jnp.where(hit, neg, cand)
    idx_ref[...] = idx
    wgt_ref[...] = sel / jnp.sum(sel, axis=0, keepdims=True) * ROUTED_SCALE


def _router(x, w_router_t, bias_col, tm):
    n = x.shape[0]
    full = lambda a: pl.BlockSpec(a.shape, lambda i: (0,) * a.ndim)
    out = pl.BlockSpec((TOP_K, tm), lambda i: (0, i))
    return pl.pallas_call(
        _router_kernel,
        out_shape=(jax.ShapeDtypeStruct((TOP_K, n), jnp.int32), jax.ShapeDtypeStruct((TOP_K, n), F32)),
        grid=(n // tm,),
        in_specs=[pl.BlockSpec((tm, D_MODEL), lambda i: (i, 0)), full(w_router_t), full(bias_col)],
        out_specs=(out, out),
        compiler_params=_cparams(("parallel",)),
        name="router",
    )(x, w_router_t, bias_col)


ROWS_PER_ISSUE = 8


def _experts_kernel(pe_ref, pb_ref, lo_ref, hi_ref, tok_ref, dst_ref, nxt_ref, x_hbm, wt_ref, wg_ref, wu_ref, wd_ref,
                    y_hbm, xin, yout, sem_in, sem_out, *, tb, n_assign):
    pair = pl.program_id(0)
    n_used = pe_ref[pe_ref.shape[0] - 1]
    slot = pair % 2

    def gather(ids_ref, buf):
        def body(o, _):
            for j in range(ROWS_PER_ISSUE):
                i = o * ROWS_PER_ISSUE + j
                pltpu.make_async_copy(x_hbm.at[pl.ds(ids_ref[0, i], 1)], xin.at[buf, pl.ds(i, 1)],
                                      sem_in.at[buf]).start(priority=j % 2)
            return 0

        lax.fori_loop(0, tb // ROWS_PER_ISSUE, body, 0)

    def wait_scatter():
        pltpu.make_async_copy(yout, y_hbm.at[pl.ds(0, tb)], sem_out).wait()

    @pl.when(pair < n_used)
    def _():
        @pl.when(pair == 0)
        def _():
            gather(tok_ref, 0)

        pltpu.make_async_copy(x_hbm.at[pl.ds(0, tb)], xin.at[slot], sem_in.at[slot]).wait()

        @pl.when(pair + 1 < n_used)
        def _():
            gather(nxt_ref, 1 - slot)

        xb = xin[slot].astype(BF16)
        hg = jnp.dot(xb, wg_ref[...].astype(BF16), preferred_element_type=F32)
        hu = jnp.dot(xb, wu_ref[...].astype(BF16), preferred_element_type=F32)
        hidden = hg * _sigmoid(hg) * hu
        y = jnp.dot(hidden.astype(BF16), wd_ref[...].astype(BF16), preferred_element_type=F32) * wt_ref[...]

        @pl.when(pair > 0)
        def _():
            wait_scatter()

        yout[...] = y
        lo, hi = lo_ref[pair], hi_ref[pair]

        def scatter(o, _):
            for j in range(ROWS_PER_ISSUE):
                i = o * ROWS_PER_ISSUE + j
                row = jnp.where((i >= lo) & (i < hi), dst_ref[0, i], n_assign + i)
                pltpu.make_async_copy(yout.at[pl.ds(i, 1)], y_hbm.at[pl.ds(row, 1)], sem_out).start(priority=j % 2)
            return 0

        lax.fori_loop(0, tb // ROWS_PER_ISSUE, scatter, 0)

        @pl.when(pair == n_used - 1)
        def _():
            wait_scatter()


def _experts(x, pairs, tok_sorted, a_sorted, w_sorted, w_e_gate, w_e_up, w_e_down, *, tb):
    pair_expert, pair_block, lo, hi = pairs
    n_pairs = pair_block.shape[0]
    n_assign = a_sorted.shape[0] * tb
    wspec = lambda a: pl.BlockSpec((None,) + a.shape[1:], lambda i, pe, pb, lo, hi: (pe[i], 0, 0))
    ids = pl.BlockSpec((None, 1, tb), lambda i, pe, pb, lo, hi: (pb[i], 0, 0), memory_space=pltpu.SMEM)
    ids_next = pl.BlockSpec((None, 1, tb), lambda i, pe, pb, lo, hi: (pb[jnp.minimum(i + 1, n_pairs - 1)], 0, 0),
                            memory_space=pltpu.SMEM)
    grid_spec = pltpu.PrefetchScalarGridSpec(
        num_scalar_prefetch=4,
        grid=(n_pairs,),
        in_specs=[ids, ids, ids_next, pl.BlockSpec(memory_space=pl.ANY),
                  pl.BlockSpec((tb, 1), lambda i, pe, pb, lo, hi: (pb[i], 0)),
                  wspec(w_e_gate), wspec(w_e_up), wspec(w_e_down)],
        out_specs=pl.BlockSpec(memory_space=pl.ANY),
        scratch_shapes=[pltpu.VMEM((2, tb, D_MODEL), F32), pltpu.VMEM((tb, D_MODEL), F32),
                        pltpu.SemaphoreType.DMA((2,)), pltpu.SemaphoreType.DMA(())],
    )
    return pl.pallas_call(
        functools.partial(_experts_kernel, tb=tb, n_assign=n_assign),
        out_shape=jax.ShapeDtypeStruct((n_assign + tb, D_MODEL), F32),
        grid_spec=grid_spec,
        compiler_params=_cparams(("arbitrary",)),
        name="experts",
    )(pair_expert, pair_block, lo, hi, tok_sorted, a_sorted, tok_sorted, x, w_sorted, w_e_gate, w_e_up, w_e_down)


def _dispatch(top_idx, top_w, n_tok, tb):
    n_assign = TOP_K * n_tok
    n_blocks = n_assign // tb
    n_pairs = n_blocks + N_EXPERTS
    a_idx = jnp.arange(n_assign, dtype=jnp.int32)
    e_sorted, a_sorted, w_sorted = lax.sort((top_idx.reshape(-1), a_idx, top_w.reshape(-1)), num_keys=1)
    experts = jnp.arange(N_EXPERTS, dtype=jnp.int32)
    start = jnp.searchsorted(e_sorted, experts, side="left").astype(jnp.int32)
    end = jnp.searchsorted(e_sorted, experts, side="right").astype(jnp.int32)
    visits = jnp.where(end > start, (end - 1) // tb - start // tb + 1, 0)
    visit_end = jnp.cumsum(visits)
    pair = jnp.arange(n_pairs, dtype=jnp.int32)
    pair_expert = jnp.minimum(jnp.searchsorted(visit_end, pair, side="right"), N_EXPERTS - 1).astype(jnp.int32)
    pair_block = start[pair_expert] // tb + pair - (visit_end - visits)[pair_expert]
    pair_block = jnp.clip(pair_block, 0, n_blocks - 1).astype(jnp.int32)
    lo = jnp.clip(start[pair_expert] - pair_block * tb, 0, tb).astype(jnp.int32)
    hi = jnp.clip(end[pair_expert] - pair_block * tb, 0, tb).astype(jnp.int32)
    n_used = visit_end[-1].astype(jnp.int32)
    pairs = (jnp.concatenate([pair_expert, n_used[None]]), pair_block, lo, hi)
    blocks = lambda v: v.reshape(n_blocks, 1, tb)
    return pairs, blocks(a_sorted % n_tok), blocks(a_sorted), w_sorted.reshape(n_assign, 1)


def _ffn_out_kernel(x_ref, wg_ref, wu_ref, wd_ref, g_ref, b_ref, *rest):
    yr_refs, o_ref = rest[:-1], rest[-1]
    x = x_ref[...]
    xb = x.astype(BF16)
    hg = jnp.dot(xb, wg_ref[...], preferred_element_type=F32)
    hu = jnp.dot(xb, wu_ref[...], preferred_element_type=F32)
    ffn = jnp.dot((hg * _sigmoid(hg) * hu).astype(BF16), wd_ref[...], preferred_element_type=F32)
    for yr_ref in yr_refs:
        ffn = ffn + yr_ref[...]
    o_ref[...] = _layer_norm(ALPHA * x + ffn, g_ref[...], b_ref[...])


def _ffn_out(x, y_routed, w_s_gate, w_s_up, w_s_down, ln_g, ln_b, tm):
    n = x.shape[0]
    full = lambda a: pl.BlockSpec(a.shape, lambda i: (0,) * a.ndim)
    row = pl.BlockSpec((tm, D_MODEL), lambda i: (i, 0))
    slab = lambda j: pl.BlockSpec((tm, D_MODEL), lambda i, j=j: (j * (n // tm) + i, 0))
    return pl.pallas_call(
        _ffn_out_kernel,
        out_shape=jax.ShapeDtypeStruct((n, D_MODEL), F32),
        grid=(n // tm,),
        in_specs=[row, full(w_s_gate), full(w_s_up), full(w_s_down), full(ln_g), full(ln_b)]
        + [slab(j) for j in range(TOP_K)],
        out_specs=row,
        compiler_params=_cparams(("parallel",)),
        name="ffn_out",
    )(x, w_s_gate, w_s_up, w_s_down, ln_g, ln_b, *([y_routed] * TOP_K))


WKV_CHUNK = 64
SB_TILE = 384
SB_HEAD_PAIRS = 2
SEQ_PAD = 768
ROW_TILE = 512
EXPERT_ROWS = 128
ROUTER_TILE = 256
SAMPLE_PAGES_PER_STEP = 8


def _pages_keys_minor(cache):
    n_pool = cache.shape[0]
    return jnp.transpose(cache, (0, 2, 3, 1)).reshape(n_pool, WIDTH, PAGE_SIZE)


def _shift_rows(p_rwkv, first):
    return jnp.concatenate([first[:, None, :], p_rwkv[:, :-1]], axis=1)


def kernel(x_prompt, x_sample, cache_k, cache_v, page_table, state_wkv, state_shift, meta_tokens, w_in, mu_shift, w0,
           w_decay_up, a0, w_a_up, w_g_up, k_k, k_a, r_k, gn_g, gn_b, sb_bias, w_proj_a, w_proj_b, w_o, ln1_g, ln1_b,
           w_router, router_bias, w_e_gate, w_e_up, w_e_down, w_s_gate, w_s_up, w_s_down, ln2_g, ln2_b):
    assert w_in.shape[0] == DEPTH == 1
    n_p, seq = x_prompt.shape[:2]
    n_s, t_s = x_sample.shape[:2]
    lp = N_META + seq
    lp_pad = -(-lp // SEQ_PAD) * SEQ_PAD
    rows_p = n_p * lp_pad
    rows_s = n_s * t_s
    n_tok = rows_p + rows_s
    assert lp_pad % WKV_CHUNK == 0 and lp_pad % SB_TILE == 0 and n_tok % ROW_TILE == 0 and rows_p % t_s == 0

    meta = jnp.broadcast_to(meta_tokens[None].astype(F32), (n_p, N_META, D_MODEL))
    xp = jnp.concatenate([meta, x_prompt, jnp.zeros((n_p, lp_pad - lp, D_MODEL), F32)], axis=1)
    x = jnp.concatenate([xp.reshape(rows_p, D_MODEL), x_sample.reshape(rows_s, D_MODEL)], axis=0)
    valid = jnp.concatenate([jnp.broadcast_to((jnp.arange(lp_pad) < lp)[None], (n_p, lp_pad)).reshape(rows_p),
                             jnp.ones((rows_s,), jnp.bool_)]).astype(F32)[:, None]

    w = w_in[0]
    w_cols = jnp.concatenate([w[:, :3 * WIDTH], w[:, P_RWKV:], w[:, 3 * WIDTH:P_RWKV]], axis=1)
    p = _project(x, w_cols.astype(BF16), ROW_TILE, 768)
    p_rwkv = jnp.concatenate([p[:, :3 * WIDTH], p[:, COL_LORA:]], axis=1)
    pp3 = p_rwkv[:rows_p].reshape(n_p, lp_pad, P_RWKV)
    ps3 = p_rwkv[rows_p:].reshape(n_s, t_s, P_RWKV)

    p_prev = jnp.concatenate([
        _shift_rows(pp3, jnp.zeros((n_p, P_RWKV), F32)).reshape(rows_p, P_RWKV),
        _shift_rows(ps3, state_shift[0]).reshape(rows_s, P_RWKV)], axis=0)
    zpad = lambda w, lo: jnp.zeros((LORA, WIDTH), F32).at[lo:lo + w.shape[0]].set(w)
    vec = lambda a: a.reshape(1, -1)
    r, lw, k, v, a, g = _rwkv_pre(p, p_prev, valid, mu_shift, vec(w0[0]), zpad(w_decay_up[0], 0), vec(a0[0]),
                                  zpad(w_a_up[0], DECAY_LORA), zpad(w_g_up[0], DECAY_LORA + A_LORA), ROW_TILE)
    head_vecs = (vec(k_k[0]), vec(k_a[0]), vec(r_k[0]), vec(gn_g[0]), vec(gn_b[0]))
    ya_p, wkv_p = _wkv_scan(r, lw, k, v, a, g, jnp.zeros((n_p,) + state_wkv.shape[2:], F32), *head_vecs,
                            row0=0, seq_stride=lp_pad, n_seq=n_p, n_chunks=lp_pad // WKV_CHUNK, chunk=WKV_CHUNK,
                            heads=8, passes=3)
    ya_s, wkv_s = _wkv_scan(r, lw, k, v, a, g, state_wkv[0], *head_vecs,
                            row0=rows_p, seq_stride=t_s, n_seq=n_s, n_chunks=1, chunk=t_s, heads=N_HEADS, passes=3)
    y_a = jnp.concatenate([ya_p, ya_s], axis=0)

    qkv_s = p[rows_p:, COL_SB:COL_GATE].reshape(rows_s, 3, WIDTH)
    yb_p = _sb_prompt(p, p[:, COL_SB + WIDTH:COL_GATE].astype(BF16), sb_bias[0], n_seq=n_p, seq_stride=lp_pad,
                      tile=SB_TILE, pairs=SB_HEAD_PAIRS)
    yb_s = _sb_sample(qkv_s[:, 0], qkv_s[:, 1], qkv_s[:, 2], jnp.repeat(sb_bias[0], t_s)[:, None],
                      _pages_keys_minor(cache_k[0]), _pages_keys_minor(cache_v[0]),
                      page_table, n_pg=SAMPLE_PAGES_PER_STEP)
    y_b = jnp.concatenate([yb_p, yb_s], axis=0)

    x1 = _merge(x, y_a, y_b, p, w_proj_a[0].astype(BF16), w_proj_b[0].astype(BF16), w_o[0].astype(BF16),
                vec(ln1_g[0]), vec(ln1_b[0]), ROW_TILE)

    top_idx, top_w = _router(x1, w_router[0].T, router_bias[0][:, None], ROUTER_TILE)
    pairs, tok_sorted, a_sorted, w_sorted = _dispatch(top_idx, top_w, n_tok, EXPERT_ROWS)
    y_routed = _experts(x1, pairs, tok_sorted, a_sorted, w_sorted, w_e_gate[0], w_e_up[0], w_e_down[0], tb=EXPERT_ROWS)
    x2 = _ffn_out(x1, y_routed, w_s_gate[0].astype(BF16), w_s_up[0].astype(BF16), w_s_down[0].astype(BF16),
                  vec(ln2_g[0]), vec(ln2_b[0]), ROW_TILE)

    x2p = x2[:rows_p].reshape(n_p, lp_pad, D_MODEL)
    kv_p = p[:rows_p, COL_SB + WIDTH:COL_GATE].reshape(n_p, lp_pad, 2, N_HEADS, HEAD_DIM)[:, :lp]
    qkv_s5 = qkv_s.reshape(n_s, t_s, 3, N_HEADS, HEAD_DIM)
    return (x2p[:, N_META:lp], x2[rows_p:].reshape(n_s, t_s, D_MODEL),
            kv_p[None, :, :, 0], kv_p[None, :, :, 1], qkv_s5[None, :, :, 1], qkv_s5[None, :, :, 2],
            wkv_p[None], pp3[:, lp - 1][None], wkv_s[None], ps3[:, t_s - 1][None])
```
